```python
import math
import jax, jax.numpy as jnp
from jax import lax
import numpy as np


D_MODEL = 1024
BATCH = 2
SEQ = 16384
DEPTH = 2

N_A = max(1, DEPTH // 2)
N_B = DEPTH - N_A
HGRN_HEADS = 8
HGRN_DK = D_MODEL // HGRN_HEADS
HGRN_DV = D_MODEL // HGRN_HEADS
HGRN_FDIM = HGRN_HEADS * HGRN_DK
HGRN_CHUNK = 16
SB_HEADS = 8
SB_HEAD_DIM = 128
SB_DIM = SB_HEADS * SB_HEAD_DIM
SB_QBLOCK = 128
PEER_HEADS = 4
PEER_NKEYS = 128
PEER_N = PEER_NKEYS * PEER_NKEYS
PEER_DK = 256
PEER_TOPK = 16
PEER_TOKEN_BLOCK = 128
NORM_EPS = 1e-6

kernel_name = 'hgrn2_stickbreak_peer_yoco'


def rms_norm(x, g):
    xf = x.astype(jnp.float32)
    y = xf * lax.rsqrt(jnp.mean(xf * xf, axis=-1, keepdims=True) + NORM_EPS)
    return (y * g.astype(jnp.float32)).astype(x.dtype)


def modulate(xn, shift, scale):
    return xn * (1 + scale[:, None, :]) + shift[:, None, :]


def hgrn2_mixer(xn, w_in, lb, onorm_g, w_out):
    b, s, _ = xn.shape
    nc = s // HGRN_CHUNK
    proj = xn @ w_in
    q, f, i, g = jnp.split(proj, [HGRN_FDIM, 2 * HGRN_FDIM, 2 * HGRN_FDIM + D_MODEL], axis=-1)
    lbf = lb.astype(jnp.float32)
    fgate = lbf + (1 - lbf) * jax.nn.sigmoid(f.astype(jnp.float32))
    log_f = jnp.log(fgate)
    k = 1 - fgate
    q = jax.nn.silu(q.astype(jnp.float32))
    v = i.astype(jnp.float32)

    def to_chunks(t, d):
        return t.reshape(b, nc, HGRN_CHUNK, HGRN_HEADS, d).transpose(1, 0, 3, 2, 4)

    causal = jnp.tril(jnp.ones((HGRN_CHUNK, HGRN_CHUNK), dtype=bool))

    def step(state, inp):
        qc, kc, vc, lfc = inp
        cum = jnp.cumsum(lfc, axis=2)
        last = cum[:, :, -1:, :]
        qd = qc * jnp.exp(cum)
        kd = kc * jnp.exp(-cum)
        scores = jnp.where(causal, jnp.einsum('bhtk,bhsk->bhts', qd, kd), 0.0)
        o = (jnp.einsum('bhts,bhsv->bhtv', scores, vc)
             + jnp.einsum('bhtk,bhkv->bhtv', qd, state))
        new_state = (jnp.exp(last[:, :, 0, :])[..., None] * state
                     + jnp.einsum('bhsk,bhsv->bhkv', kc * jnp.exp(last - cum), vc))
        return new_state, o

    state0 = jnp.zeros((b, HGRN_HEADS, HGRN_DK, HGRN_DV), jnp.float32)
    _, o = lax.scan(step, state0, (to_chunks(q, HGRN_DK), to_chunks(k, HGRN_DK),
                                   to_chunks(v, HGRN_DV), to_chunks(log_f, HGRN_DK)))
    o = o.transpose(1, 0, 3, 2, 4).reshape(b, s, HGRN_HEADS, HGRN_DV)
    o = o * lax.rsqrt(jnp.mean(o * o, axis=-1, keepdims=True) + NORM_EPS)
    o = o.reshape(b, s, D_MODEL) * onorm_g.astype(jnp.float32) * jax.nn.silu(g.astype(jnp.float32))
    return o.astype(xn.dtype) @ w_out


def shared_kv(h, c_act, kv_ada_w, kv_ada_b, kv_norm_g, kv_w):
    b, s, _ = h.shape
    shift, scale = jnp.split(c_act @ kv_ada_w + kv_ada_b, 2, axis=-1)
    hn = modulate(rms_norm(h, kv_norm_g), shift, scale)
    kv = (hn @ kv_w).reshape(b, s, 2, SB_HEADS, SB_HEAD_DIM)
    k = kv[:, :, 0].transpose(0, 2, 1, 3)
    v = kv[:, :, 1].transpose(0, 2, 1, 3)
    return k, v


def stick_breaking_mixer(xn, k_sh, v_sh, w_q, w_out):
    b, s, _ = xn.shape
    nq = s // SB_QBLOCK
    q = (xn @ w_q).reshape(b, s, SB_HEADS, SB_HEAD_DIM).transpose(0, 2, 1, 3) * (1.0 / math.sqrt(SB_HEAD_DIM))
    loc = jnp.arange(SB_QBLOCK)
    tri = (loc[:, None] >= loc[None, :]).astype(jnp.float32)
    outs = []
    for i in range(nq):
        nk = i + 1
        qi = q[:, :, i * SB_QBLOCK:(i + 1) * SB_QBLOCK]
        kb = k_sh[:, :, :nk * SB_QBLOCK].reshape(b, SB_HEADS, nk, SB_QBLOCK, SB_HEAD_DIM)
        vb = v_sh[:, :, :nk * SB_QBLOCK].reshape(b, SB_HEADS, nk, SB_QBLOCK, SB_HEAD_DIM)
        z = jnp.einsum('bhqd,bhnsd->bhqns', qi, kb).astype(jnp.float32)
        s_pos = jnp.arange(nk * SB_QBLOCK).reshape(nk, SB_QBLOCK)
        t_pos = i * SB_QBLOCK + loc
        mask = s_pos[None, :, :] < t_pos[:, None, None]
        log_keep = jnp.where(mask, jax.nn.log_sigmoid(-z), 0.0)
        cs_in = jnp.einsum('bhqnj,js->bhqns', log_keep, tri)
        blk = cs_in[..., 0]
        after = lax.cumsum(blk, axis=3, reverse=True) - blk
        a = jnp.where(mask, jnp.exp(z + cs_in + after[..., None]), 0.0)
        outs.append(jnp.einsum('bhqns,bhnsd->bhqd', a.astype(vb.dtype), vb))
    o = jnp.concatenate(outs, axis=2)
    o = o.transpose(0, 2, 1, 3).reshape(b, s, SB_DIM)
    return o @ w_out


def peer_ffn(xn, w_q, subkeys, u, v):
    b, s, d = xn.shape
    t = b * s
    nb = t // PEER_TOKEN_BLOCK
    xt = xn.reshape(t, d)
    qr = (xt @ w_q).reshape(t, PEER_HEADS, 2, PEER_DK // 2)
    sc = jnp.einsum('thpd,hpnd->thpn', qr, subkeys).astype(jnp.float32)
    s1, i1 = lax.top_k(sc[:, :, 0], PEER_TOPK)
    s2, i2 = lax.top_k(sc[:, :, 1], PEER_TOPK)
    cand = (s1[..., :, None] + s2[..., None, :]).reshape(t, PEER_HEADS, PEER_TOPK * PEER_TOPK)
    best, flat = lax.top_k(cand, PEER_TOPK)
    e1 = jnp.take_along_axis(i1, flat // PEER_TOPK, axis=-1)
    e2 = jnp.take_along_axis(i2, flat % PEER_TOPK, axis=-1)
    experts = (e1 * PEER_NKEYS + e2).reshape(nb, PEER_TOKEN_BLOCK, PEER_HEADS * PEER_TOPK)
    gates = jax.nn.softmax(best, axis=-1).reshape(nb, PEER_TOKEN_BLOCK, PEER_HEADS * PEER_TOPK)

    def block(args):
        xc, ec, gc = args
        hid = jnp.einsum('cd,ced->ce', xc, jnp.take(u, ec, axis=0)).astype(jnp.float32)
        w = (gc * jax.nn.gelu(hid, approximate=False)).astype(xc.dtype)
        return jnp.einsum('ce,ced->cd', w, jnp.take(v, ec, axis=0))

    out = lax.map(block, (xt.reshape(nb, PEER_TOKEN_BLOCK, d), experts, gates))
    return out.reshape(b, s, d)


def setup_inputs(seed: int = 0) -> dict:
    key = jax.random.key(seed)
    ks = jax.random.split(key, 22)
    D = D_MODEL

    def nrm(k, shape, std):
        return jax.random.normal(k, shape, jnp.float32) * std

    return {
        'x': nrm(ks[0], (BATCH, SEQ, D), 1.0),
        'c': nrm(ks[1], (BATCH, D), 1.0),
        'ada_w': nrm(ks[2], (DEPTH, D, 6 * D), 0.5 * D ** -0.5),
        'ada_b': nrm(ks[3], (DEPTH, 6 * D), 0.02),
        'norm_mix_g': 1.0 + nrm(ks[4], (DEPTH, D), 0.02),
        'norm_ffn_g': 1.0 + nrm(ks[5], (DEPTH, D), 0.02),
        'hgrn_w_in': nrm(ks[6], (N_A, D, 2 * HGRN_FDIM + 2 * D), D ** -0.5),
        'hgrn_lb_logits': nrm(ks[7], (N_A + 1, HGRN_FDIM), 0.5),
        'hgrn_onorm_g': 1.0 + nrm(ks[8], (N_A, D), 0.02),
        'hgrn_w_out': nrm(ks[9], (N_A, D, D), D ** -0.5),
        'kv_ada_w': nrm(ks[10], (D, 2 * D), 0.5 * D ** -0.5),
        'kv_ada_b': nrm(ks[11], (2 * D,), 0.02),
        'kv_norm_g': 1.0 + nrm(ks[12], (D,), 0.02),
        'kv_w': nrm(ks[13], (D, 2 * SB_DIM), D ** -0.5),
        'sb_w_q': nrm(ks[14], (N_B, D, SB_DIM), D ** -0.5),
        'sb_w_out': nrm(ks[15], (N_B, SB_DIM, D), SB_DIM ** -0.5),
        'peer_w_q': nrm(ks[16], (DEPTH, D, PEER_HEADS * PEER_DK), D ** -0.5),
        'peer_subkeys': nrm(ks[17], (DEPTH, PEER_HEADS, 2, PEER_NKEYS, PEER_DK // 2), (PEER_DK // 2) ** -0.5),
        'peer_u': nrm(ks[18], (DEPTH, PEER_N, D), D ** -0.5),
        'peer_v': nrm(ks[19], (DEPTH, PEER_N, D), PEER_HEADS ** -0.5),
        'final_norm_g': 1.0 + nrm(ks[20], (D,), 0.02),
    }


def reference(x, c, ada_w, ada_b, norm_mix_g, norm_ffn_g, hgrn_w_in, hgrn_lb_logits,
              hgrn_onorm_g, hgrn_w_out, kv_ada_w, kv_ada_b, kv_norm_g, kv_w, sb_w_q,
              sb_w_out, peer_w_q, peer_subkeys, peer_u, peer_v, final_norm_g):
    c_act = jax.nn.silu(c)
    lb_all = jnp.cumsum(jax.nn.softmax(hgrn_lb_logits.astype(jnp.float32), axis=0), axis=0)
    h = x
    k_sh = None
    v_sh = None
    for l in range(DEPTH):
        mod = c_act @ ada_w[l] + ada_b[l]
        sh1, sc1, g1, sh2, sc2, g2 = jnp.split(mod, 6, axis=-1)
        hn = modulate(rms_norm(h, norm_mix_g[l]), sh1, sc1)
        if l < N_A:
            mix = hgrn2_mixer(hn, hgrn_w_in[l], lb_all[l], hgrn_onorm_g[l], hgrn_w_out[l])
        else:
            mix = stick_breaking_mixer(hn, k_sh, v_sh, sb_w_q[l - N_A], sb_w_out[l - N_A])
        h = h + g1[:, None, :] * mix
        hn = modulate(rms_norm(h, norm_ffn_g[l]), sh2, sc2)
        h = h + g2[:, None, :] * peer_ffn(hn, peer_w_q[l], peer_subkeys[l], peer_u[l], peer_v[l])
        if l == N_A - 1:
            k_sh, v_sh = shared_kv(h, c_act, kv_ada_w, kv_ada_b, kv_norm_g, kv_w)
    return rms_norm(h, final_norm_g)
```

```python
import functools
import math

import jax
import jax.numpy as jnp
from jax import lax
from jax.experimental import pallas as pl
from jax.experimental.pallas import tpu as pltpu

F32 = jnp.float32
BF16 = jnp.bfloat16
HIGHEST = lax.Precision.HIGHEST

NORM_EPS = 1e-6
HEAD_DIM = 128
HGRN_CHUNK = 32
SB_BLOCK = 128
PEER_HEADS = 4
PEER_NKEYS = 128
PEER_TOPK = 16
SB_LOG_FLOOR = -100.0

VMEM_LIMIT = 56 * 1024 * 1024


def _cparams(*sem):
    return pltpu.CompilerParams(dimension_semantics=sem, vmem_limit_bytes=VMEM_LIMIT)


def _rms_mod(h, g, shift, scale):
    ms = jnp.mean(h * h, axis=-1, keepdims=True)
    return (h * lax.rsqrt(ms + NORM_EPS) * g) * (1.0 + scale) + shift


def _gelu(x):
    return 0.5 * x * (1.0 + lax.erf(x * (1.0 / math.sqrt(2.0))))


def _mod_kernel(c_ref, w_ref, b_ref, o_ref):
    c = c_ref[...]
    ca = c * jax.nn.sigmoid(c)
    o_ref[...] = jnp.dot(ca, w_ref[...], precision=HIGHEST, preferred_element_type=F32) + b_ref[...]


def _modulation(c8, w3, b3, layer, tn=1024):
    _, d, n = w3.shape
    return pl.pallas_call(
        _mod_kernel,
        out_shape=jax.ShapeDtypeStruct((8, n), F32),
        grid=(n // tn,),
        in_specs=[
            pl.BlockSpec((8, d), lambda j: (0, 0)),
            pl.BlockSpec((None, d, tn), lambda j: (layer, 0, j)),
            pl.BlockSpec((None, 1, tn), lambda j: (layer, 0, j)),
        ],
        out_specs=pl.BlockSpec((8, tn), lambda j: (0, j)),
        compiler_params=_cparams("arbitrary"),
        name="adaln_modulation",
    )(c8, w3, b3)


def _hgrn_kernel(h_ref, g_ref, sh_ref, sc_ref, gate_ref, win_ref, lbl_ref, og_ref, wout_ref,
                 o_ref, q_s, k_s, lf_s, v_s, gg_s, ob_s, st_s, *, layer, ts, nh):
    d = h_ref.shape[-1]
    c = HGRN_CHUNK

    @pl.when(pl.program_id(1) == 0)
    def _():
        st_s[...] = jnp.zeros_like(st_s)

    h = h_ref[...]
    hn = _rms_mod(h, g_ref[...], sh_ref[...], sc_ref[...])
    proj = jnp.dot(hn.astype(BF16), win_ref[...], preferred_element_type=F32)
    lbl = lbl_ref[...]
    e = jnp.exp(lbl - jnp.max(lbl, axis=0, keepdims=True))
    lb = jnp.sum(e[: layer + 1], axis=0, keepdims=True) / jnp.sum(e, axis=0, keepdims=True)
    fg = lb + (1.0 - lb) * jax.nn.sigmoid(proj[:, d:2 * d])
    q = proj[:, :d]
    q_s[...] = q * jax.nn.sigmoid(q)
    k_s[...] = 1.0 - fg
    lf_s[...] = jnp.log(fg)
    v_s[...] = proj[:, 2 * d:3 * d]
    gg = proj[:, 3 * d:]
    gg_s[...] = gg * jax.nn.sigmoid(gg)

    row = lax.broadcasted_iota(jnp.int32, (c, c), 0)
    col = lax.broadcasted_iota(jnp.int32, (c, c), 1)
    causal = row >= col
    tril = causal.astype(F32)

    def chunk(ci, carry):
        r0 = pl.multiple_of(ci * c, c)
        lf = lf_s[pl.ds(r0, c), :]
        qc = q_s[pl.ds(r0, c), :]
        kc = k_s[pl.ds(r0, c), :]
        vc = v_s[pl.ds(r0, c), :].astype(BF16)
        cum = jnp.dot(tril, lf, precision=HIGHEST, preferred_element_type=F32)
        mid = cum[c // 2 - 1:c // 2, :]
        last = cum[c - 1:c, :]
        qh = (qc * jnp.exp(cum - mid)).astype(BF16)
        kh = (kc * jnp.exp(mid - cum)).astype(BF16)
        qs = (qc * jnp.exp(cum)).astype(BF16)
        kl = (kc * jnp.exp(last - cum)).astype(BF16)
        dec = jnp.exp(last)
        for hd in range(nh):
            sl = slice(hd * HEAD_DIM, (hd + 1) * HEAD_DIM)
            sc = lax.dot_general(qh[:, sl], kh[:, sl], (((1,), (1,)), ((), ())), preferred_element_type=F32)
            sc = jnp.where(causal, sc, 0.0).astype(BF16)
            st = st_s[hd]
            o = jnp.dot(sc, vc[:, sl], preferred_element_type=F32)
            o = o + lax.dot_general(qs[:, sl], st.astype(BF16), (((1,), (1,)), ((), ())),
                                    preferred_element_type=F32)
            ob_s[pl.ds(r0, c), sl] = o
            st_s[hd] = st * dec[:, sl] + lax.dot_general(vc[:, sl], kl[:, sl], (((0,), (0,)), ((), ())),
                                                         preferred_element_type=F32)
        return carry

    lax.fori_loop(0, ts // c, chunk, 0)

    og = og_ref[...]
    gg = gg_s[...]
    parts = []
    for hd in range(nh):
        sl = slice(hd * HEAD_DIM, (hd + 1) * HEAD_DIM)
        o = ob_s[:, sl]
        o = o * lax.rsqrt(jnp.mean(o * o, axis=-1, keepdims=True) + NORM_EPS)
        parts.append((o * og[:, sl] * gg[:, sl]).astype(BF16))
    on = jnp.concatenate(parts, axis=1)
    mix = jnp.dot(on, wout_ref[...], preferred_element_type=F32)
    o_ref[...] = h + gate_ref[...] * mix


def _hgrn_layer(h, g, sh, sc, gate, w_in, lb_logits, onorm_g, w_out, layer, ts=256):
    b, s, d = h.shape
    nh = d // HEAD_DIM
    row = lambda bi, ti: (bi, 0, 0)
    const2 = lambda bi, ti: (0, 0)
    return pl.pallas_call(
        functools.partial(_hgrn_kernel, layer=layer, ts=ts, nh=nh),
        out_shape=jax.ShapeDtypeStruct((b, s, d), F32),
        grid=(b, s // ts),
        in_specs=[
            pl.BlockSpec((None, ts, d), lambda bi, ti: (bi, ti, 0)),
            pl.BlockSpec((1, d), const2),
            pl.BlockSpec((None, 1, d), row),
            pl.BlockSpec((None, 1, d), row),
            pl.BlockSpec((None, 1, d), row),
            pl.BlockSpec(w_in.shape, const2),
            pl.BlockSpec(lb_logits.shape, const2),
            pl.BlockSpec((1, d), const2),
            pl.BlockSpec(w_out.shape, const2),
        ],
        out_specs=pl.BlockSpec((None, ts, d), lambda bi, ti: (bi, ti, 0)),
        scratch_shapes=[pltpu.VMEM((ts, d), F32)] * 6 + [pltpu.VMEM((nh, HEAD_DIM, HEAD_DIM), F32)],
        compiler_params=_cparams("arbitrary", "arbitrary"),
        name="hgrn2_layer",
    )(h, g, sh, sc, gate, w_in, lb_logits, onorm_g, w_out)


def _proj_heads_kernel(h_ref, g_ref, sh_ref, sc_ref, w_ref, o_ref, *, out_scale):
    hn = _rms_mod(h_ref[...], g_ref[...], sh_ref[...], sc_ref[...])
    y = jnp.dot(hn.astype(BF16), w_ref[...], preferred_element_type=F32)
    if out_scale != 1.0:
        y = y * out_scale
    for j in range(o_ref.shape[0]):
        o_ref[j] = y[:, j * HEAD_DIM:(j + 1) * HEAD_DIM].astype(o_ref.dtype)


def _proj_heads(h, g, sh, sc, w, out_scale=1.0, tm=512):
    b, s, d = h.shape
    tm = min(tm, s)
    n = w.shape[1]
    nj = n // HEAD_DIM
    row = lambda bi, ti: (bi, 0, 0)
    const2 = lambda bi, ti: (0, 0)
    return pl.pallas_call(
        functools.partial(_proj_heads_kernel, out_scale=out_scale),
        out_shape=jax.ShapeDtypeStruct((b, nj, s, HEAD_DIM), BF16),
        grid=(b, s // tm),
        in_specs=[
            pl.BlockSpec((None, tm, d), lambda bi, ti: (bi, ti, 0)),
            pl.BlockSpec((1, d), const2),
            pl.BlockSpec((None, 1, d), row),
            pl.BlockSpec((None, 1, d), row),
            pl.BlockSpec(w.shape, const2),
        ],
        out_specs=pl.BlockSpec((None, nj, tm, HEAD_DIM), lambda bi, ti: (bi, 0, ti, 0)),
        compiler_params=_cparams("arbitrary", "arbitrary"),
        name="norm_mod_proj",
    )(h, g, sh, sc, w)


def _sb_attn_kernel(q_ref, k_ref, v_ref, o_ref, *, s):
    qb = SB_BLOCK
    nq = s // qb
    row = lax.broadcasted_iota(jnp.int32, (qb, qb), 0)
    col = lax.broadcasted_iota(jnp.int32, (qb, qb), 1)
    strict = col < row
    tri = (row >= col).astype(BF16)

    def kmax_body(j, m):
        kb = k_ref[pl.ds(pl.multiple_of(j * qb, qb), qb), :].astype(F32)
        return jnp.maximum(m, jnp.max(jnp.sum(kb * kb, axis=1, keepdims=True)))
    knorm = jnp.sqrt(lax.fori_loop(0, nq, kmax_body, jnp.float32(0.0)))

    def block(q, j, after, acc, masked):
        r0 = pl.multiple_of(j * qb, qb)
        kb = k_ref[pl.ds(r0, qb), :]
        vb = v_ref[pl.ds(r0, qb), :]
        z = lax.dot_general(q, kb, (((1,), (1,)), ((), ())), preferred_element_type=F32)
        lk = jax.nn.log_sigmoid(-z)
        if masked:
            lk = jnp.where(strict, lk, 0.0)
        hi = lk.astype(BF16)
        lo = (lk - hi.astype(F32)).astype(BF16)
        cs = (jnp.dot(hi, tri, preferred_element_type=F32)
              + jnp.dot(lo, tri, preferred_element_type=F32))
        a = jnp.exp(z + cs + after)
        if masked:
            a = jnp.where(strict, a, 0.0)
        acc = acc + jnp.dot(a.astype(BF16), vb, preferred_element_type=F32)
        return after + cs[:, 0:1], acc

    def qblock(i, carry):
        q0 = pl.multiple_of(i * qb, qb)
        q = q_ref[pl.ds(q0, qb), :]
        qf = q.astype(F32)
        zbound = jnp.sqrt(jnp.sum(qf * qf, axis=1, keepdims=True)) * knorm
        after, acc = block(q, i, jnp.zeros((qb, 1), F32), jnp.zeros((qb, HEAD_DIM), F32), True)

        def cond(st):
            j, after, _ = st
            return jnp.logical_and(j >= 0, jnp.max(zbound + after) > SB_LOG_FLOOR)

        def body(st):
            j, after, acc = st
            after, acc = block(q, j, after, acc, False)
            return j - 1, after, acc

        _, _, acc = lax.while_loop(cond, body, (i - 1, after, acc))
        o_ref[pl.ds(q0, qb), :] = acc.astype(o_ref.dtype)
        return carry

    lax.fori_loop(0, nq, qblock, 0)


def _sb_attention(q, kv):
    b, nh, s, dh = q.shape
    blk = (None, None, s, dh)
    return pl.pallas_call(
        functools.partial(_sb_attn_kernel, s=s),
        out_shape=jax.ShapeDtypeStruct((b, nh, s, dh), BF16),
        grid=(b, nh),
        in_specs=[
            pl.BlockSpec(blk, lambda bi, hi: (bi, hi, 0, 0)),
            pl.BlockSpec(blk, lambda bi, hi: (bi, hi, 0, 0)),
            pl.BlockSpec(blk, lambda bi, hi: (bi, nh + hi, 0, 0)),
        ],
        out_specs=pl.BlockSpec(blk, lambda bi, hi: (bi, hi, 0, 0)),
        compiler_params=_cparams("arbitrary", "arbitrary"),
        name="stick_breaking_attention",
    )(q, kv, kv)


def _out_proj_kernel(a_ref, w_ref, h_ref, gate_ref, o_ref):
    a = jnp.concatenate([a_ref[j] for j in range(a_ref.shape[0])], axis=1)
    mix = jnp.dot(a, w_ref[...], preferred_element_type=F32)
    o_ref[...] = h_ref[...] + gate_ref[...] * mix


def _out_proj_residual(a, w, h, gate, tm=512):
    b, s, d = h.shape
    tm = min(tm, s)
    nh = a.shape[1]
    return pl.pallas_call(
        _out_proj_kernel,
        out_shape=jax.ShapeDtypeStruct((b, s, d), F32),
        grid=(b, s // tm),
        in_specs=[
            pl.BlockSpec((None, nh, tm, HEAD_DIM), lambda bi, ti: (bi, 0, ti, 0)),
            pl.BlockSpec(w.shape, lambda bi, ti: (0, 0)),
            pl.BlockSpec((None, tm, d), lambda bi, ti: (bi, ti, 0)),
            pl.BlockSpec((None, 1, d), lambda bi, ti: (bi, 0, 0)),
        ],
        out_specs=pl.BlockSpec((None, tm, d), lambda bi, ti: (bi, ti, 0)),
        compiler_params=_cparams("arbitrary", "arbitrary"),
        name="out_proj_residual",
    )(a, w, h, gate)


def _top16(s, ids):
    n = s.shape[0]
    iota = lax.broadcasted_iota(jnp.int32, s.shape, 0)
    vals, picked = [], []
    for _ in range(PEER_TOPK):
        m = jnp.max(s, axis=0, keepdims=True)
        first = jnp.min(jnp.where(s == m, iota, n), axis=0, keepdims=True)
        hit = iota == first
        vals.append(m)
        picked.append(jnp.sum(jnp.where(hit, ids, 0), axis=0, keepdims=True))
        s = jnp.where(hit, -jnp.inf, s)
    return jnp.concatenate(vals, axis=0), jnp.concatenate(picked, axis=0)


def _route_kernel(h_ref, g_ref, sh_ref, sc_ref, wq_ref, sk_ref, e_ref, gt_ref):
    hn = _rms_mod(h_ref[...], g_ref[...], sh_ref[...], sc_ref[...])
    qr = jnp.dot(hn, wq_ref[...], precision=HIGHEST, preferred_element_type=F32)
    tm = qr.shape[0]
    key_ids = lax.broadcasted_iota(jnp.int32, (PEER_NKEYS, tm), 0)
    for hd in range(PEER_HEADS):
        tops = []
        for p in range(2):
            c0 = (hd * 2 + p) * HEAD_DIM
            sc = lax.dot_general(sk_ref[hd, p], qr[:, c0:c0 + HEAD_DIM], (((1,), (1,)), ((), ())),
                                 precision=HIGHEST, preferred_element_type=F32)
            tops.append(_top16(sc, key_ids))
        (s1, i1), (s2, i2) = tops
        cand = jnp.concatenate([s1[i:i + 1, :] + s2 for i in range(PEER_TOPK)], axis=0)
        eid = jnp.concatenate([i1[i:i + 1, :] * PEER_NKEYS + i2 for i in range(PEER_TOPK)], axis=0)
        best, experts = _top16(cand, eid)
        ex = jnp.exp(best - best[0:1, :])
        gates = ex / jnp.sum(ex, axis=0, keepdims=True)
        e_ref[hd * PEER_TOPK:(hd + 1) * PEER_TOPK, :] = experts
        gt_ref[hd * PEER_TOPK:(hd + 1) * PEER_TOPK, :] = gates


def _peer_route(h2, g, sh, sc, w_q, subkeys, s, tm=256):
    t, d = h2.shape
    tm = min(tm, s)
    ne = PEER_HEADS * PEER_TOPK
    per_b = s // tm
    row = lambda i: (i // per_b, 0, 0)
    return pl.pallas_call(
        _route_kernel,
        out_shape=(jax.ShapeDtypeStruct((ne, t), jnp.int32), jax.ShapeDtypeStruct((ne, t), F32)),
        grid=(t // tm,),
        in_specs=[
            pl.BlockSpec((tm, d), lambda i: (i, 0)),
            pl.BlockSpec((1, d), lambda i: (0, 0)),
            pl.BlockSpec((None, 1, d), row),
            pl.BlockSpec((None, 1, d), row),
            pl.BlockSpec(w_q.shape, lambda i: (0, 0)),
            pl.BlockSpec(subkeys.shape, lambda i: (0, 0, 0, 0)),
        ],
        out_specs=(pl.BlockSpec((ne, tm), lambda i: (0, i)), pl.BlockSpec((ne, tm), lambda i: (0, i))),
        compiler_params=_cparams("arbitrary"),
        name="peer_route",
    )(h2, g, sh, sc, w_q, subkeys)


def _expert_kernel(idx_ref, idxn_ref, gt_ref, h_ref, g_ref, sh_ref, sc_ref, gate_ref, fg_ref, tab_ref,
                   o_ref, buf, sem, *, tb, ne, final_norm):
    i = pl.program_id(0)
    n = pl.num_programs(0)
    slot = i % 2
    d = h_ref.shape[-1]
    rows = tb * ne

    def gather_copy(idx, r, sl):
        return pltpu.make_async_copy(tab_ref.at[idx], buf.at[sl, r], sem.at[sl])

    def issue(ids_ref, sl):
        for t in range(tb):
            def body(e, c):
                gather_copy(ids_ref[e, t], t * ne + e, sl).start()
                return c
            lax.fori_loop(0, ne, body, 0, unroll=8)

    @pl.when(i == 0)
    def _():
        issue(idx_ref, 0)

    @pl.when(i + 1 < n)
    def _():
        issue(idxn_ref, 1 - slot)

    pltpu.make_async_copy(tab_ref.at[pl.ds(0, rows)], buf.at[slot], sem.at[slot]).wait()

    h = h_ref[...]
    hn = _rms_mod(h, g_ref[...], sh_ref[...], sc_ref[...])
    gt = gt_ref[...]
    outs = []
    for t in range(tb):
        u = buf[slot, t * ne:(t + 1) * ne, :d]
        v = buf[slot, t * ne:(t + 1) * ne, d:]
        hid = jnp.sum(u * hn[t:t + 1, :], axis=1, keepdims=True)
        w = gt[:, t:t + 1] * _gelu(hid)
        outs.append(jnp.sum(w * v, axis=0, keepdims=True))
    y = h + gate_ref[...] * jnp.concatenate(outs, axis=0)
    if final_norm:
        y = y * lax.rsqrt(jnp.mean(y * y, axis=-1, keepdims=True) + NORM_EPS) * fg_ref[...]
    o_ref[...] = y


def _peer_experts(h2, experts, gates, g, sh, sc, gate, final_g, table, s, final_norm, tb=16):
    t, d = h2.shape
    ne = experts.shape[0]
    nblk = t // tb
    per_b = s // tb
    row = lambda i: (i // per_b, 0, 0)
    experts = experts.reshape(ne, nblk, tb).transpose(1, 0, 2)
    gates = gates.reshape(ne, nblk, tb).transpose(1, 0, 2)
    return pl.pallas_call(
        functools.partial(_expert_kernel, tb=tb, ne=ne, final_norm=final_norm),
        out_shape=jax.ShapeDtypeStruct((t, d), F32),
        grid=(nblk,),
        in_specs=[
            pl.BlockSpec((None, ne, tb), lambda i: (i, 0, 0), memory_space=pltpu.SMEM),
            pl.BlockSpec((None, ne, tb), lambda i: (jnp.minimum(i + 1, nblk - 1), 0, 0), memory_space=pltpu.SMEM),
            pl.BlockSpec((None, ne, tb), lambda i: (i, 0, 0)),
            pl.BlockSpec((tb, d), lambda i: (i, 0)),
            pl.BlockSpec((1, d), lambda i: (0, 0)),
            pl.BlockSpec((None, 1, d), row),
            pl.BlockSpec((None, 1, d), row),
            pl.BlockSpec((None, 1, d), row),
            pl.BlockSpec((1, d), lambda i: (0, 0)),
            pl.BlockSpec(memory_space=pl.ANY),
        ],
        out_specs=pl.BlockSpec((tb, d), lambda i: (i, 0)),
        scratch_shapes=[pltpu.VMEM((2, tb * ne, 2 * d), F32), pltpu.SemaphoreType.DMA((2,))],
        compiler_params=_cparams("arbitrary"),
        name="peer_experts",
    )(experts, experts, gates, h2, g, sh, sc, gate, final_g, table)


def kernel(x, c, ada_w, ada_b, norm_mix_g, norm_ffn_g, hgrn_w_in, hgrn_lb_logits, hgrn_onorm_g, hgrn_w_out,
           kv_ada_w, kv_ada_b, kv_norm_g, kv_w, sb_w_q, sb_w_out, peer_w_q, peer_subkeys, peer_u, peer_v,
           final_norm_g):
    b, s, d = x.shape
    depth = ada_w.shape[0]
    n_a = hgrn_w_in.shape[0]
    c8 = jnp.zeros((8, d), F32).at[:b].set(c)

    def mods(w3, b3, layer, parts):
        m = _modulation(c8, w3, b3.reshape(b3.shape[0], 1, -1), layer)[:b]
        return [p.reshape(b, 1, d) for p in jnp.split(m, parts, axis=-1)]

    h = x
    kv = None
    for l in range(depth):
        sh1, sc1, g1, sh2, sc2, g2 = mods(ada_w, ada_b, l, 6)
        gm = norm_mix_g[l].reshape(1, d)
        if l < n_a:
            h = _hgrn_layer(h, gm, sh1, sc1, g1, hgrn_w_in[l].astype(BF16), hgrn_lb_logits,
                            hgrn_onorm_g[l].reshape(1, d), hgrn_w_out[l].astype(BF16), l)
        else:
            q = _proj_heads(h, gm, sh1, sc1, sb_w_q[l - n_a].astype(BF16),
                            out_scale=1.0 / math.sqrt(HEAD_DIM))
            a = _sb_attention(q, kv)
            h = _out_proj_residual(a, sb_w_out[l - n_a].astype(BF16), h, g1)
        gf = norm_ffn_g[l].reshape(1, d)
        h2 = h.reshape(b * s, d)
        experts, gates = _peer_route(h2, gf, sh2, sc2, peer_w_q[l], peer_subkeys[l], s)
        table = jnp.concatenate([peer_u[l], peer_v[l]], axis=1)
        h2 = _peer_experts(h2, experts, gates, gf, sh2, sc2, g2, final_norm_g.reshape(1, d), table, s,
                           final_norm=(l == depth - 1))
        h = h2.reshape(b, s, d)
        if l == n_a - 1:
            ksh, ksc = mods(kv_ada_w.reshape(1, d, 2 * d), kv_ada_b.reshape(1, 2 * d), 0, 2)
            kv = _proj_heads(h, kv_norm_g.reshape(1, d), ksh, ksc, kv_w.astype(BF16))
    return h
```

```python
import functools
import math

import jax
import jax.numpy as jnp
from jax import lax
from jax.experimental import pallas as pl
from jax.experimental.pallas import tpu as pltpu

F32 = jnp.float32
BF16 = jnp.bfloat16
HIGHEST = lax.Precision.HIGHEST

NORM_EPS = 1e-6
HEAD_DIM = 128
HGRN_CHUNK = 32
SB_BLOCK = 128
PEER_HEADS = 4
PEER_NKEYS = 128
PEER_TOPK = 16
SB_LOG_FLOOR = -100.0

VMEM_LIMIT = 56 * 1024 * 1024


def _cparams(*sem):
    return pltpu.CompilerParams(dimension_semantics=sem, vmem_limit_bytes=VMEM_LIMIT)


def _rms_mod(h, g, shift, scale):
    ms = jnp.mean(h * h, axis=-1, keepdims=True)
    return (h * lax.rsqrt(ms + NORM_EPS) * g) * (1.0 + scale) + shift


def _gelu(x):
    return 0.5 * x * (1.0 + lax.erf(x * (1.0 / math.sqrt(2.0))))


def _mod_kernel(c_ref, w_ref, b_ref, o_ref):
    c = c_ref[...]
    ca = c * jax.nn.sigmoid(c)
    o_ref[...] = jnp.dot(ca, w_ref[...], precision=HIGHEST, preferred_element_type=F32) + b_ref[...]


def _modulation(c8, w3, b3, layer, tn=1024):
    _, d, n = w3.shape
    return pl.pallas_call(
        _mod_kernel,
        out_shape=jax.ShapeDtypeStruct((8, n), F32),
        grid=(n // tn,),
        in_specs=[
            pl.BlockSpec((8, d), lambda j: (0, 0)),
            pl.BlockSpec((None, d, tn), lambda j: (layer, 0, j)),
            pl.BlockSpec((None, 1, tn), lambda j: (layer, 0, j)),
        ],
        out_specs=pl.BlockSpec((8, tn), lambda j: (0, j)),
        compiler_params=_cparams("arbitrary"),
        name="adaln_modulation",
    )(c8, w3, b3)


def _hgrn_kernel(h_ref, g_ref, sh_ref, sc_ref, gate_ref, win_ref, lbl_ref, og_ref, wout_ref,
                 o_ref, q_s, k_s, lf_s, v_s, gg_s, ob_s, st_s, *, layer, ts, nh):
    d = h_ref.shape[-1]
    c = HGRN_CHUNK

    @pl.when(pl.program_id(1) == 0)
    def _():
        st_s[...] = jnp.zeros_like(st_s)

    h = h_ref[...]
    hn = _rms_mod(h, g_ref[...], sh_ref[...], sc_ref[...])
    proj = jnp.dot(hn.astype(BF16), win_ref[...], preferred_element_type=F32)
    lbl = lbl_ref[...]
    e = jnp.exp(lbl - jnp.max(lbl, axis=0, keepdims=True))
    lb = jnp.sum(e[: layer + 1], axis=0, keepdims=True) / jnp.sum(e, axis=0, keepdims=True)
    fg = lb + (1.0 - lb) * jax.nn.sigmoid(proj[:, d:2 * d])
    q = proj[:, :d]
    q_s[...] = q * jax.nn.sigmoid(q)
    k_s[...] = 1.0 - fg
    lf_s[...] = jnp.log(fg)
    v_s[...] = proj[:, 2 * d:3 * d]
    gg = proj[:, 3 * d:]
    gg_s[...] = gg * jax.nn.sigmoid(gg)

    row = lax.broadcasted_iota(jnp.int32, (c, c), 0)
    col = lax.broadcasted_iota(jnp.int32, (c, c), 1)
    causal = row >= col
    tril = causal.astype(F32)

    def chunk(ci, carry):
        r0 = pl.multiple_of(ci * c, c)
        lf = lf_s[pl.ds(r0, c), :]
        qc = q_s[pl.ds(r0, c), :]
        kc = k_s[pl.ds(r0, c), :]
        vc = v_s[pl.ds(r0, c), :].astype(BF16)
        cum = jnp.dot(tril, lf, precision=HIGHEST, preferred_element_type=F32)
        mid = cum[c // 2 - 1:c // 2, :]
        last = cum[c - 1:c, :]
        qh = (qc * jnp.exp(cum - mid)).astype(BF16)
        kh = (kc * jnp.exp(mid - cum)).astype(BF16)
        qs = (qc * jnp.exp(cum)).astype(BF16)
        kl = (kc * jnp.exp(last - cum)).astype(BF16)
        dec = jnp.exp(last)
        for hd in range(nh):
            sl = slice(hd * HEAD_DIM, (hd + 1) * HEAD_DIM)
            sc = lax.dot_general(qh[:, sl], kh[:, sl], (((1,), (1,)), ((), ())), preferred_element_type=F32)
            sc = jnp.where(causal, sc, 0.0).astype(BF16)
            st = st_s[hd]
            o = jnp.dot(sc, vc[:, sl], preferred_element_type=F32)
            o = o + lax.dot_general(qs[:, sl], st.astype(BF16), (((1,), (1,)), ((), ())),
                                    preferred_element_type=F32)
            ob_s[pl.ds(r0, c), sl] = o
            st_s[hd] = st * dec[:, sl] + lax.dot_general(vc[:, sl], kl[:, sl], (((0,), (0,)), ((), ())),
                                                         preferred_element_type=F32)
        return carry

    lax.fori_loop(0, ts // c, chunk, 0)

    og = og_ref[...]
    gg = gg_s[...]
    parts = []
    for hd in range(nh):
        sl = slice(hd * HEAD_DIM, (hd + 1) * HEAD_DIM)
        o = ob_s[:, sl]
        o = o * lax.rsqrt(jnp.mean(o * o, axis=-1, keepdims=True) + NORM_EPS)
        parts.append((o * og[:, sl] * gg[:, sl]).astype(BF16))
    on = jnp.concatenate(parts, axis=1)
    mix = jnp.dot(on, wout_ref[...], preferred_element_type=F32)
    o_ref[...] = h + gate_ref[...] * mix


def _hgrn_layer(h, g, sh, sc, gate, w_in, lb_logits, onorm_g, w_out, layer, ts=256):
    b, s, d = h.shape
    nh = d // HEAD_DIM
    row = lambda bi, ti: (bi, 0, 0)
    const2 = lambda bi, ti: (0, 0)
    return pl.pallas_call(
        functools.partial(_hgrn_kernel, layer=layer, ts=ts, nh=nh),
        out_shape=jax.ShapeDtypeStruct((b, s, d), F32),
        grid=(b, s // ts),
        in_specs=[
            pl.BlockSpec((None, ts, d), lambda bi, ti: (bi, ti, 0)),
            pl.BlockSpec((1, d), const2),
            pl.BlockSpec((None, 1, d), row),
            pl.BlockSpec((None, 1, d), row),
            pl.BlockSpec((None, 1, d), row),
            pl.BlockSpec(w_in.shape, const2),
            pl.BlockSpec(lb_logits.shape, const2),
            pl.BlockSpec((1, d), const2),
            pl.BlockSpec(w_out.shape, const2),
        ],
        out_specs=pl.BlockSpec((None, ts, d), lambda bi, ti: (bi, ti, 0)),
        scratch_shapes=[pltpu.VMEM((ts, d), F32)] * 6 + [pltpu.VMEM((nh, HEAD_DIM, HEAD_DIM), F32)],
        compiler_params=_cparams("arbitrary", "arbitrary"),
        name="hgrn2_layer",
    )(h, g, sh, sc, gate, w_in, lb_logits, onorm_g, w_out)


def _proj_heads_kernel(h_ref, g_ref, sh_ref, sc_ref, w_ref, o_ref, *, out_scale):
    hn = _rms_mod(h_ref[...], g_ref[...], sh_ref[...], sc_ref[...])
    y = jnp.dot(hn.astype(BF16), w_ref[...], preferred_element_type=F32)
    if out_scale != 1.0:
        y = y * out_scale
    for j in range(o_ref.shape[0]):
        o_ref[j] = y[:, j * HEAD_DIM:(j + 1) * HEAD_DIM].astype(o_ref.dtype)


def _proj_heads(h, g, sh, sc, w, out_scale=1.0, tm=512):
    b, s, d = h.shape
    tm = min(tm, s)
    n = w.shape[1]
    nj = n // HEAD_DIM
    row = lambda bi, ti: (bi, 0, 0)
    const2 = lambda bi, ti: (0, 0)
    return pl.pallas_call(
        functools.partial(_proj_heads_kernel, out_scale=out_scale),
        out_shape=jax.ShapeDtypeStruct((b, nj, s, HEAD_DIM), BF16),
        grid=(b, s // tm),
        in_specs=[
            pl.BlockSpec((None, tm, d), lambda bi, ti: (bi, ti, 0)),
            pl.BlockSpec((1, d), const2),
            pl.BlockSpec((None, 1, d), row),
            pl.BlockSpec((None, 1, d), row),
            pl.BlockSpec(w.shape, const2),
        ],
        out_specs=pl.BlockSpec((None, nj, tm, HEAD_DIM), lambda bi, ti: (bi, 0, ti, 0)),
        compiler_params=_cparams("arbitrary", "arbitrary"),
        name="norm_mod_proj",
    )(h, g, sh, sc, w)


SB_HEADS_PER_STEP = 2
SB_QTILE = 2048


def _sb_attn_kernel(q_ref, k_ref, v_ref, o_ref, kn_ref, *, s, tq):
    qb = SB_BLOCK
    nh = q_ref.shape[0]
    row = lax.broadcasted_iota(jnp.int32, (qb, qb), 0)
    col = lax.broadcasted_iota(jnp.int32, (qb, qb), 1)
    strict = col < row
    tri = (row >= col).astype(BF16)
    ones = jnp.ones((qb, qb), BF16)

    @pl.when(pl.program_id(2) == 0)
    def _():
        rows = min(s, 8 * qb)
        for hd in range(nh):
            def kmax_body(j, m, hd=hd):
                kb = k_ref[hd, pl.ds(pl.multiple_of(j * rows, rows), rows), :].astype(F32)
                ssq = jnp.dot((kb * kb).astype(BF16), ones, preferred_element_type=F32)
                return jnp.maximum(m, jnp.max(ssq, axis=0, keepdims=True))
            m = lax.fori_loop(0, s // rows, kmax_body, jnp.zeros((1, qb), F32))
            kn_ref[hd] = jnp.sqrt(jnp.max(m)) * 1.01

    def block(hd, q, j, after, acc, masked):
        r0 = pl.multiple_of(j * qb, qb)
        kb = k_ref[hd, pl.ds(r0, qb), :]
        vb = v_ref[hd, pl.ds(r0, qb), :]
        z = lax.dot_general(q, kb, (((1,), (1,)), ((), ())), preferred_element_type=F32)
        lk = jax.nn.log_sigmoid(-z)
        if masked:
            lk = jnp.where(strict, lk, 0.0)
        hi = lk.astype(BF16)
        lo = (lk - hi.astype(F32)).astype(BF16)
        cs = (jnp.dot(hi, tri, preferred_element_type=F32)
              + jnp.dot(lo, tri, preferred_element_type=F32))
        a = jnp.exp(z + cs + after)
        if masked:
            a = jnp.where(strict, a, 0.0)
        acc = acc + jnp.dot(a.astype(BF16), vb, preferred_element_type=F32)
        return after + cs[:, 0:1], acc

    def qblock(i, carry):
        q0 = pl.multiple_of(i * qb, qb)
        gi = pl.program_id(2) * (tq // qb) + i
        qs, zbounds, afters, accs = [], [], [], []
        for hd in range(nh):
            q = q_ref[hd, pl.ds(q0, qb), :]
            qf = q.astype(F32)
            zbounds.append(jnp.sqrt(jnp.sum(qf * qf, axis=1, keepdims=True)) * kn_ref[hd])
            after, acc = block(hd, q, gi, jnp.zeros((qb, 1), F32), jnp.zeros((qb, HEAD_DIM), F32), True)
            qs.append(q)
            afters.append(after)
            accs.append(acc)

        def cond(st):
            j, afters, _ = st
            live = jnp.max(zbounds[0] + afters[0])
            for hd in range(1, nh):
                live = jnp.maximum(live, jnp.max(zbounds[hd] + afters[hd]))
            return jnp.logical_and(j >= 0, live > SB_LOG_FLOOR)

        def body(st):
            j, afters, accs = st
            out = [block(hd, qs[hd], j, afters[hd], accs[hd], False) for hd in range(nh)]
            return j - 1, tuple(o[0] for o in out), tuple(o[1] for o in out)

        _, _, accs = lax.while_loop(cond, body, (gi - 1, tuple(afters), tuple(accs)))
        for hd in range(nh):
            o_ref[hd, pl.ds(q0, qb), :] = accs[hd].astype(o_ref.dtype)
        return carry

    lax.fori_loop(0, tq // qb, qblock, 0)


def _sb_attention(q, kv):
    b, nh, s, dh = q.shape
    hp = SB_HEADS_PER_STEP
    tq = min(SB_QTILE, s)
    qblk = (None, hp, tq, dh)
    kblk = (None, hp, s, dh)
    return pl.pallas_call(
        functools.partial(_sb_attn_kernel, s=s, tq=tq),
        out_shape=jax.ShapeDtypeStruct((b, nh, s, dh), BF16),
        grid=(b, nh // hp, s // tq),
        in_specs=[
            pl.BlockSpec(qblk, lambda bi, hi, qi: (bi, hi, qi, 0)),
            pl.BlockSpec(kblk, lambda bi, hi, qi: (bi, hi, 0, 0)),
            pl.BlockSpec(kblk, lambda bi, hi, qi: (bi, nh // hp + hi, 0, 0)),
        ],
        out_specs=pl.BlockSpec(qblk, lambda bi, hi, qi: (bi, hi, qi, 0)),
        scratch_shapes=[pltpu.SMEM((hp,), F32)],
        compiler_params=_cparams("arbitrary", "arbitrary", "arbitrary"),
        name="stick_breaking_attention",
    )(q, kv, kv)


def _out_proj_kernel(a_ref, w_ref, h_ref, gate_ref, o_ref):
    a = jnp.concatenate([a_ref[j] for j in range(a_ref.shape[0])], axis=1)
    mix = jnp.dot(a, w_ref[...], preferred_element_type=F32)
    o_ref[...] = h_ref[...] + gate_ref[...] * mix


def _out_proj_residual(a, w, h, gate, tm=512):
    b, s, d = h.shape
    tm = min(tm, s)
    nh = a.shape[1]
    return pl.pallas_call(
        _out_proj_kernel,
        out_shape=jax.ShapeDtypeStruct((b, s, d), F32),
        grid=(b, s // tm),
        in_specs=[
            pl.BlockSpec((None, nh, tm, HEAD_DIM), lambda bi, ti: (bi, 0, ti, 0)),
            pl.BlockSpec(w.shape, lambda bi, ti: (0, 0)),
            pl.BlockSpec((None, tm, d), lambda bi, ti: (bi, ti, 0)),
            pl.BlockSpec((None, 1, d), lambda bi, ti: (bi, 0, 0)),
        ],
        out_specs=pl.BlockSpec((None, tm, d), lambda bi, ti: (bi, ti, 0)),
        compiler_params=_cparams("arbitrary", "arbitrary"),
        name="out_proj_residual",
    )(a, w, h, gate)


def _top16(s, ids=None):
    n = s.shape[0]
    iota = lax.broadcasted_iota(jnp.int32, s.shape, 0).astype(F32)
    vals, picked = [], []
    for _ in range(PEER_TOPK):
        m = jnp.max(s, axis=0, keepdims=True)
        first = jnp.min(jnp.where(s == m, iota, float(n)), axis=0, keepdims=True)
        hit = iota == first
        vals.append(m)
        picked.append(first if ids is None else jnp.sum(jnp.where(hit, ids, 0), axis=0, keepdims=True))
        s = jnp.where(hit, -jnp.inf, s)
    return jnp.concatenate(vals, axis=0), jnp.concatenate(picked, axis=0).astype(jnp.int32)


_PAIR_COUNTS = tuple(PEER_TOPK // (i + 1) for i in range(PEER_TOPK))
_PAIR_PAD = -sum(_PAIR_COUNTS) % 8


def _route_kernel(h_ref, g_ref, sh_ref, sc_ref, wqh_ref, wql_ref, sk_ref, e_ref, gt_ref):
    hn = _rms_mod(h_ref[...], g_ref[...], sh_ref[...], sc_ref[...])
    hn_hi = hn.astype(BF16)
    hn_lo = (hn - hn_hi.astype(F32)).astype(BF16)
    qr = (jnp.dot(hn_hi, wqh_ref[...], preferred_element_type=F32)
          + (jnp.dot(hn_hi, wql_ref[...], preferred_element_type=F32)
             + jnp.dot(hn_lo, wqh_ref[...], preferred_element_type=F32)))
    tm = qr.shape[0]
    for hd in range(PEER_HEADS):
        tops = []
        for p in range(2):
            c0 = (hd * 2 + p) * HEAD_DIM
            sc = lax.dot_general(sk_ref[hd, p], qr[:, c0:c0 + HEAD_DIM], (((1,), (1,)), ((), ())),
                                 precision=HIGHEST, preferred_element_type=F32)
            tops.append(_top16(sc))
        (s1, i1), (s2, i2) = tops
        cand = jnp.concatenate([s1[i:i + 1, :] + s2[:c, :] for i, c in enumerate(_PAIR_COUNTS)]
                               + [jnp.full((_PAIR_PAD, tm), -jnp.inf, F32)], axis=0)
        eid = jnp.concatenate([i1[i:i + 1, :] * PEER_NKEYS + i2[:c, :] for i, c in enumerate(_PAIR_COUNTS)]
                              + [jnp.zeros((_PAIR_PAD, tm), jnp.int32)], axis=0)
        best, experts = _top16(cand, eid)
        ex = jnp.exp(best - best[0:1, :])
        gates = ex / jnp.sum(ex, axis=0, keepdims=True)
        e_ref[hd * PEER_TOPK:(hd + 1) * PEER_TOPK, :] = experts
        gt_ref[hd * PEER_TOPK:(hd + 1) * PEER_TOPK, :] = gates


def _peer_route(h2, g, sh, sc, w_q, subkeys, s, tm=128):
    t, d = h2.shape
    tm = min(tm, s)
    ne = PEER_HEADS * PEER_TOPK
    per_b = s // tm
    row = lambda i: (i // per_b, 0, 0)
    w_hi = w_q.astype(BF16)
    w_lo = (w_q - w_hi.astype(F32)).astype(BF16)
    return pl.pallas_call(
        _route_kernel,
        out_shape=(jax.ShapeDtypeStruct((ne, t), jnp.int32), jax.ShapeDtypeStruct((ne, t), F32)),
        grid=(t // tm,),
        in_specs=[
            pl.BlockSpec((tm, d), lambda i: (i, 0)),
            pl.BlockSpec((1, d), lambda i: (0, 0)),
            pl.BlockSpec((None, 1, d), row),
            pl.BlockSpec((None, 1, d), row),
            pl.BlockSpec(w_q.shape, lambda i: (0, 0)),
            pl.BlockSpec(w_q.shape, lambda i: (0, 0)),
            pl.BlockSpec(subkeys.shape, lambda i: (0, 0, 0, 0)),
        ],
        out_specs=(pl.BlockSpec((ne, tm), lambda i: (0, i)), pl.BlockSpec((ne, tm), lambda i: (0, i))),
        compiler_params=_cparams("arbitrary"),
        name="peer_route",
    )(h2, g, sh, sc, w_hi, w_lo, subkeys)


PEER_RING = 4
PEER_GROUP = 8
PEER_PITCH = 20


def _expert_kernel(idx_ref, idxn_ref, gt_ref, h_ref, g_ref, sh_ref, sc_ref, gate_ref, fg_ref, tab_ref,
                   o_ref, *scratch, tg, ne, final_norm):
    bufs, sem = scratch[:PEER_RING], scratch[PEER_RING]
    i = pl.program_id(0)
    n = pl.num_programs(0)
    d = h_ref.shape[-1]
    rows = tg * ne
    ahead = PEER_RING // 2

    nc = d // HEAD_DIM
    rc = 2 * nc

    def start_token(ids_ref, grp, t, k):
        for e in range(ne):
            src = tab_ref.at[pl.ds(pl.multiple_of(ids_ref[e, grp * tg + t], rc), rc), :]
            pltpu.make_async_copy(src, bufs[k].at[pl.ds((t * ne + e) * PEER_PITCH, rc), :],
                                  sem.at[k]).start(priority=e % 2)

    def wait_group(k):
        pltpu.make_async_copy(tab_ref.at[pl.ds(0, rows * rc)], bufs[k].at[pl.ds(0, rows * rc)], sem.at[k]).wait()

    @pl.when(i == 0)
    def _():
        for k in range(ahead):
            for t in range(tg):
                start_token(idx_ref, k, t, k)

    h = h_ref[...]
    hn = _rms_mod(h, g_ref[...], sh_ref[...], sc_ref[...])
    gt = gt_ref[...]
    outs = []
    for k in range(PEER_RING):
        wait_group(k)
        nk = (k + ahead) % PEER_RING
        ids_ref = idx_ref if k + ahead < PEER_RING else idxn_ref
        for t in range(tg):
            start_token(ids_ref, nk, t, nk)
            tok = k * tg + t

            def chunk(g, c):
                return bufs[k][pl.ds(((t * ne + g * 8) * PEER_PITCH + c), 8, stride=PEER_PITCH), :]

            ws = []
            for g in range(ne // 8):
                acc = chunk(g, 0) * hn[tok:tok + 1, 0:HEAD_DIM]
                for c in range(1, nc):
                    acc = acc + chunk(g, c) * hn[tok:tok + 1, c * HEAD_DIM:(c + 1) * HEAD_DIM]
                hid = jnp.sum(acc, axis=1, keepdims=True)
                ws.append(gt[g * 8:(g + 1) * 8, tok:tok + 1] * _gelu(hid))
            pieces = []
            for c in range(nc):
                acc = ws[0] * chunk(0, nc + c)
                for g in range(1, ne // 8):
                    acc = acc + ws[g] * chunk(g, nc + c)
                pieces.append(jnp.sum(acc, axis=0, keepdims=True))
            outs.append(jnp.concatenate(pieces, axis=1))
    y = h + gate_ref[...] * jnp.concatenate(outs, axis=0)
    if final_norm:
        y = y * lax.rsqrt(jnp.mean(y * y, axis=-1, keepdims=True) + NORM_EPS) * fg_ref[...]
    o_ref[...] = y

    @pl.when(i == n - 1)
    def _():
        for k in range(ahead):
            wait_group(k)


def _peer_experts(h2, experts, gates, g, sh, sc, gate, final_g, table, s, final_norm):
    t, d = h2.shape
    ne = experts.shape[0]
    tg = PEER_GROUP
    tb = PEER_RING * tg
    nblk = t // tb
    per_b = s // tb
    row = lambda i: (i // per_b, 0, 0)
    rc = 2 * d // HEAD_DIM
    experts = (experts * rc).reshape(ne, nblk, tb).transpose(1, 0, 2)
    gates = gates.reshape(ne, nblk, tb).transpose(1, 0, 2)
    return pl.pallas_call(
        functools.partial(_expert_kernel, tg=tg, ne=ne, final_norm=final_norm),
        out_shape=jax.ShapeDtypeStruct((t, d), F32),
        grid=(nblk,),
        in_specs=[
            pl.BlockSpec((None, ne, tb), lambda i: (i, 0, 0), memory_space=pltpu.SMEM),
            pl.BlockSpec((None, ne, tb), lambda i: (jnp.minimum(i + 1, nblk - 1), 0, 0), memory_space=pltpu.SMEM),
            pl.BlockSpec((None, ne, tb), lambda i: (i, 0, 0)),
            pl.BlockSpec((tb, d), lambda i: (i, 0)),
            pl.BlockSpec((1, d), lambda i: (0, 0)),
            pl.BlockSpec((None, 1, d), row),
            pl.BlockSpec((None, 1, d), row),
            pl.BlockSpec((None, 1, d), row),
            pl.BlockSpec((1, d), lambda i: (0, 0)),
            pl.BlockSpec(memory_space=pl.ANY),
        ],
        out_specs=pl.BlockSpec((tb, d), lambda i: (i, 0)),
        scratch_shapes=([pltpu.VMEM((tg * ne * PEER_PITCH, HEAD_DIM), F32)] * PEER_RING
                        + [pltpu.SemaphoreType.DMA((PEER_RING,))]),
        compiler_params=_cparams("arbitrary"),
        name="peer_experts",
    )(experts, experts, gates, h2, g, sh, sc, gate, final_g, table)


def kernel(x, c, ada_w, ada_b, norm_mix_g, norm_ffn_g, hgrn_w_in, hgrn_lb_logits, hgrn_onorm_g, hgrn_w_out,
           kv_ada_w, kv_ada_b, kv_norm_g, kv_w, sb_w_q, sb_w_out, peer_w_q, peer_subkeys, peer_u, peer_v,
           final_norm_g):
    b, s, d = x.shape
    depth = ada_w.shape[0]
    n_a = hgrn_w_in.shape[0]
    c8 = jnp.zeros((8, d), F32).at[:b].set(c)

    def mods(w3, b3, layer, parts):
        m = _modulation(c8, w3, b3.reshape(b3.shape[0], 1, -1), layer)[:b]
        return [p.reshape(b, 1, d) for p in jnp.split(m, parts, axis=-1)]

    h = x
    kv = None
    for l in range(depth):
        sh1, sc1, g1, sh2, sc2, g2 = mods(ada_w, ada_b, l, 6)
        gm = norm_mix_g[l].reshape(1, d)
        if l < n_a:
            h = _hgrn_layer(h, gm, sh1, sc1, g1, hgrn_w_in[l].astype(BF16), hgrn_lb_logits,
                            hgrn_onorm_g[l].reshape(1, d), hgrn_w_out[l].astype(BF16), l)
        else:
            q = _proj_heads(h, gm, sh1, sc1, sb_w_q[l - n_a].astype(BF16),
                            out_scale=1.0 / math.sqrt(HEAD_DIM))
            a = _sb_attention(q, kv)
            h = _out_proj_residual(a, sb_w_out[l - n_a].astype(BF16), h, g1)
        gf = norm_ffn_g[l].reshape(1, d)
        h2 = h.reshape(b * s, d)
        experts, gates = _peer_route(h2, gf, sh2, sc2, peer_w_q[l], peer_subkeys[l], s)
        n_exp = peer_u.shape[1]
        table = jnp.concatenate([peer_u[l].reshape(n_exp, -1, HEAD_DIM), peer_v[l].reshape(n_exp, -1, HEAD_DIM)],
                                axis=1).reshape(-1, HEAD_DIM)
        h2 = _peer_experts(h2, experts, gates, gf, sh2, sc2, g2, final_norm_g.reshape(1, d), table, s,
                           final_norm=(l == depth - 1))
        h = h2.reshape(b, s, d)
        if l == n_a - 1:
            ksh, ksc = mods(kv_ada_w.reshape(1, d, 2 * d), kv_ada_b.reshape(1, 2 * d), 0, 2)
            kv = _proj_heads(h, kv_norm_g.reshape(1, d), ksh, ksc, kv_w.astype(BF16))
    return h
```

```python
import functools
import math

import jax
import jax.numpy as jnp
from jax import lax
from jax.experimental import pallas as pl
from jax.experimental.pallas import tpu as pltpu

F32 = jnp.float32
BF16 = jnp.bfloat16
HIGHEST = lax.Precision.HIGHEST

NORM_EPS = 1e-6
HEAD_DIM = 128
HGRN_CHUNK = 32
SB_BLOCK = 128
PEER_HEADS = 4
PEER_NKEYS = 128
PEER_TOPK = 16
SB_LOG_FLOOR = -100.0

VMEM_LIMIT = 56 * 1024 * 1024


def _cparams(*sem):
    return pltpu.CompilerParams(dimension_semantics=sem, vmem_limit_bytes=VMEM_LIMIT)


def _rms_mod(h, g, shift, scale):
    ms = jnp.mean(h * h, axis=-1, keepdims=True)
    return (h * lax.rsqrt(ms + NORM_EPS) * g) * (1.0 + scale) + shift


def _gelu(x):
    return 0.5 * x * (1.0 + lax.erf(x * (1.0 / math.sqrt(2.0))))


def _mod_kernel(c_ref, w_ref, b_ref, o_ref):
    c = c_ref[...]
    ca = c * jax.nn.sigmoid(c)
    o_ref[...] = jnp.dot(ca, w_ref[...], precision=HIGHEST, preferred_element_type=F32) + b_ref[...]


def _modulation(c8, w3, b3, layer, tn=1024):
    _, d, n = w3.shape
    return pl.pallas_call(
        _mod_kernel,
        out_shape=jax.ShapeDtypeStruct((8, n), F32),
        grid=(n // tn,),
        in_specs=[
            pl.BlockSpec((8, d), lambda j: (0, 0)),
            pl.BlockSpec((None, d, tn), lambda j: (layer, 0, j)),
            pl.BlockSpec((None, 1, tn), lambda j: (layer, 0, j)),
        ],
        out_specs=pl.BlockSpec((8, tn), lambda j: (0, j)),
        compiler_params=_cparams("arbitrary"),
        name="adaln_modulation",
    )(c8, w3, b3)


def _hgrn_kernel(h_ref, g_ref, sh_ref, sc_ref, gate_ref, win_ref, lbl_ref, og_ref, wout_ref,
                 o_ref, q_s, k_s, lf_s, v_s, gg_s, ob_s, st_s, *, layer, ts, nh):
    d = h_ref.shape[-1]
    c = HGRN_CHUNK

    @pl.when(pl.program_id(1) == 0)
    def _():
        st_s[...] = jnp.zeros_like(st_s)

    h = h_ref[...]
    hn = _rms_mod(h, g_ref[...], sh_ref[...], sc_ref[...])
    proj = jnp.dot(hn.astype(BF16), win_ref[...], preferred_element_type=F32)
    lbl = lbl_ref[...]
    e = jnp.exp(lbl - jnp.max(lbl, axis=0, keepdims=True))
    lb = jnp.sum(e[: layer + 1], axis=0, keepdims=True) / jnp.sum(e, axis=0, keepdims=True)
    fg = lb + (1.0 - lb) * jax.nn.sigmoid(proj[:, d:2 * d])
    q = proj[:, :d]
    q_s[...] = q * jax.nn.sigmoid(q)
    k_s[...] = 1.0 - fg
    lf_s[...] = jnp.log(fg)
    v_s[...] = proj[:, 2 * d:3 * d]
    gg = proj[:, 3 * d:]
    gg_s[...] = gg * jax.nn.sigmoid(gg)

    row = lax.broadcasted_iota(jnp.int32, (c, c), 0)
    col = lax.broadcasted_iota(jnp.int32, (c, c), 1)
    causal = row >= col
    tril = causal.astype(F32)

    def chunk(ci, carry):
        r0 = pl.multiple_of(ci * c, c)
        lf = lf_s[pl.ds(r0, c), :]
        qc = q_s[pl.ds(r0, c), :]
        kc = k_s[pl.ds(r0, c), :]
        vc = v_s[pl.ds(r0, c), :].astype(BF16)
        cum = jnp.dot(tril, lf, precision=HIGHEST, preferred_element_type=F32)
        mid = cum[c // 2 - 1:c // 2, :]
        last = cum[c - 1:c, :]
        qh = (qc * jnp.exp(cum - mid)).astype(BF16)
        kh = (kc * jnp.exp(mid - cum)).astype(BF16)
        qs = (qc * jnp.exp(cum)).astype(BF16)
        kl = (kc * jnp.exp(last - cum)).astype(BF16)
        dec = jnp.exp(last)
        for hd in range(nh):
            sl = slice(hd * HEAD_DIM, (hd + 1) * HEAD_DIM)
            sc = lax.dot_general(qh[:, sl], kh[:, sl], (((1,), (1,)), ((), ())), preferred_element_type=F32)
            sc = jnp.where(causal, sc, 0.0).astype(BF16)
            st = st_s[hd]
            o = jnp.dot(sc, vc[:, sl], preferred_element_type=F32)
            o = o + lax.dot_general(qs[:, sl], st.astype(BF16), (((1,), (1,)), ((), ())),
                                    preferred_element_type=F32)
            ob_s[pl.ds(r0, c), sl] = o
            st_s[hd] = st * dec[:, sl] + lax.dot_general(vc[:, sl], kl[:, sl], (((0,), (0,)), ((), ())),
                                                         preferred_element_type=F32)
        return carry

    lax.fori_loop(0, ts // c, chunk, 0, unroll=2)

    og = og_ref[...]
    gg = gg_s[...]
    parts = []
    for hd in range(nh):
        sl = slice(hd * HEAD_DIM, (hd + 1) * HEAD_DIM)
        o = ob_s[:, sl]
        o = o * lax.rsqrt(jnp.mean(o * o, axis=-1, keepdims=True) + NORM_EPS)
        parts.append((o * og[:, sl] * gg[:, sl]).astype(BF16))
    on = jnp.concatenate(parts, axis=1)
    mix = jnp.dot(on, wout_ref[...], preferred_element_type=F32)
    o_ref[...] = h + gate_ref[...] * mix


def _hgrn_layer(h, g, sh, sc, gate, w_in, lb_logits, onorm_g, w_out, layer, ts=256):
    b, s, d = h.shape
    nh = d // HEAD_DIM
    row = lambda bi, ti: (bi, 0, 0)
    const2 = lambda bi, ti: (0, 0)
    return pl.pallas_call(
        functools.partial(_hgrn_kernel, layer=layer, ts=ts, nh=nh),
        out_shape=jax.ShapeDtypeStruct((b, s, d), F32),
        grid=(b, s // ts),
        in_specs=[
            pl.BlockSpec((None, ts, d), lambda bi, ti: (bi, ti, 0)),
            pl.BlockSpec((1, d), const2),
            pl.BlockSpec((None, 1, d), row),
            pl.BlockSpec((None, 1, d), row),
            pl.BlockSpec((None, 1, d), row),
            pl.BlockSpec(w_in.shape, const2),
            pl.BlockSpec(lb_logits.shape, const2),
            pl.BlockSpec((1, d), const2),
            pl.BlockSpec(w_out.shape, const2),
        ],
        out_specs=pl.BlockSpec((None, ts, d), lambda bi, ti: (bi, ti, 0)),
        scratch_shapes=[pltpu.VMEM((ts, d), F32)] * 6 + [pltpu.VMEM((nh, HEAD_DIM, HEAD_DIM), F32)],
        compiler_params=_cparams("arbitrary", "arbitrary"),
        name="hgrn2_layer",
    )(h, g, sh, sc, gate, w_in, lb_logits, onorm_g, w_out)


def _proj_heads_kernel(h_ref, g_ref, sh_ref, sc_ref, w_ref, o_ref, *, out_scale):
    hn = _rms_mod(h_ref[...], g_ref[...], sh_ref[...], sc_ref[...])
    y = jnp.dot(hn.astype(BF16), w_ref[...], preferred_element_type=F32)
    if out_scale != 1.0:
        y = y * out_scale
    for j in range(o_ref.shape[0]):
        o_ref[j] = y[:, j * HEAD_DIM:(j + 1) * HEAD_DIM].astype(o_ref.dtype)


def _proj_heads(h, g, sh, sc, w, out_scale=1.0, tm=512):
    b, s, d = h.shape
    tm = min(tm, s)
    n = w.shape[1]
    nj = n // HEAD_DIM
    row = lambda bi, ti: (bi, 0, 0)
    const2 = lambda bi, ti: (0, 0)
    return pl.pallas_call(
        functools.partial(_proj_heads_kernel, out_scale=out_scale),
        out_shape=jax.ShapeDtypeStruct((b, nj, s, HEAD_DIM), BF16),
        grid=(b, s // tm),
        in_specs=[
            pl.BlockSpec((None, tm, d), lambda bi, ti: (bi, ti, 0)),
            pl.BlockSpec((1, d), const2),
            pl.BlockSpec((None, 1, d), row),
            pl.BlockSpec((None, 1, d), row),
            pl.BlockSpec(w.shape, const2),
        ],
        out_specs=pl.BlockSpec((None, nj, tm, HEAD_DIM), lambda bi, ti: (bi, 0, ti, 0)),
        compiler_params=_cparams("arbitrary", "arbitrary"),
        name="norm_mod_proj",
    )(h, g, sh, sc, w)


SB_HEADS_PER_STEP = 2
SB_QTILE = 2048
SB_AHEAD = 2


def _sb_attn_kernel(q_ref, k_ref, v_ref, o_ref, kn_ref, *, s, tq):
    qb = SB_BLOCK
    nh = q_ref.shape[0]
    row = lax.broadcasted_iota(jnp.int32, (qb, qb), 0)
    col = lax.broadcasted_iota(jnp.int32, (qb, qb), 1)
    strict = col < row
    tri = (row >= col).astype(BF16)
    ones = jnp.ones((qb, qb), BF16)

    @pl.when(pl.program_id(2) == 0)
    def _():
        rows = min(s, 8 * qb)
        for hd in range(nh):
            def kmax_body(j, m, hd=hd):
                kb = k_ref[hd, pl.ds(pl.multiple_of(j * rows, rows), rows), :].astype(F32)
                ssq = jnp.dot((kb * kb).astype(BF16), ones, preferred_element_type=F32)
                return jnp.maximum(m, jnp.max(ssq, axis=0, keepdims=True))
            m = lax.fori_loop(0, s // rows, kmax_body, jnp.zeros((1, qb), F32))
            kn_ref[hd] = jnp.sqrt(jnp.max(m)) * 1.01

    def scores(hd, q, j, keep):
        r0 = pl.multiple_of(j * qb, qb)
        kb = k_ref[hd, pl.ds(r0, qb), :]
        vb = v_ref[hd, pl.ds(r0, qb), :]
        z = lax.dot_general(q, kb, (((1,), (1,)), ((), ())), preferred_element_type=F32)
        lk = jax.nn.log_sigmoid(-z)
        if keep is not None:
            lk = jnp.where(keep, lk, 0.0)
        hi = lk.astype(BF16)
        lo = (lk - hi.astype(F32)).astype(BF16)
        cs = (jnp.dot(hi, tri, preferred_element_type=F32)
              + jnp.dot(lo, tri, preferred_element_type=F32))
        return z, cs, vb

    def accumulate(z, cs, vb, keep, after, acc):
        a = jnp.exp(z + cs + after)
        if keep is not None:
            a = jnp.where(keep, a, 0.0)
        acc = acc + jnp.dot(a.astype(BF16), vb, preferred_element_type=F32)
        return after + cs[:, 0:1], acc

    def block(hd, q, j, after, acc):
        z, cs, vb = scores(hd, q, j, None)
        return accumulate(z, cs, vb, None, after, acc)

    def qblock(i, carry):
        q0 = pl.multiple_of(i * qb, qb)
        gi = pl.program_id(2) * (tq // qb) + i
        qs, zbounds, afters, accs = [], [], [], []
        for hd in range(nh):
            q = q_ref[hd, pl.ds(q0, qb), :]
            qf = q.astype(F32)
            zbounds.append(jnp.sqrt(jnp.sum(qf * qf, axis=1, keepdims=True)) * kn_ref[hd])
            qs.append(q)
        for hd in range(nh):
            keeps = [strict] + [gi - a >= 0 for a in range(1, SB_AHEAD + 1)]
            parts = [scores(hd, qs[hd], jnp.maximum(gi - a, 0), keeps[a]) for a in range(SB_AHEAD + 1)]
            after, acc = jnp.zeros((qb, 1), F32), jnp.zeros((qb, HEAD_DIM), F32)
            for (z, cs, vb), keep in zip(parts, keeps):
                after, acc = accumulate(z, cs, vb, keep, after, acc)
            afters.append(after)
            accs.append(acc)

        def cond(st):
            j, afters, _ = st
            live = jnp.max(zbounds[0] + afters[0])
            for hd in range(1, nh):
                live = jnp.maximum(live, jnp.max(zbounds[hd] + afters[hd]))
            return jnp.logical_and(j >= 0, live > SB_LOG_FLOOR)

        def body(st):
            j, afters, accs = st
            out = [block(hd, qs[hd], j, afters[hd], accs[hd]) for hd in range(nh)]
            return j - 1, tuple(o[0] for o in out), tuple(o[1] for o in out)

        _, _, accs = lax.while_loop(cond, body, (gi - SB_AHEAD - 1, tuple(afters), tuple(accs)))
        for hd in range(nh):
            o_ref[hd, pl.ds(q0, qb), :] = accs[hd].astype(o_ref.dtype)
        return carry

    lax.fori_loop(0, tq // qb, qblock, 0)


def _sb_attention(q, kv):
    b, nh, s, dh = q.shape
    hp = SB_HEADS_PER_STEP
    tq = min(SB_QTILE, s)
    qblk = (None, hp, tq, dh)
    kblk = (None, hp, s, dh)
    return pl.pallas_call(
        functools.partial(_sb_attn_kernel, s=s, tq=tq),
        out_shape=jax.ShapeDtypeStruct((b, nh, s, dh), BF16),
        grid=(b, nh // hp, s // tq),
        in_specs=[
            pl.BlockSpec(qblk, lambda bi, hi, qi: (bi, hi, qi, 0)),
            pl.BlockSpec(kblk, lambda bi, hi, qi: (bi, hi, 0, 0)),
            pl.BlockSpec(kblk, lambda bi, hi, qi: (bi, nh // hp + hi, 0, 0)),
        ],
        out_specs=pl.BlockSpec(qblk, lambda bi, hi, qi: (bi, hi, qi, 0)),
        scratch_shapes=[pltpu.SMEM((hp,), F32)],
        compiler_params=_cparams("arbitrary", "arbitrary", "arbitrary"),
        name="stick_breaking_attention",
    )(q, kv, kv)


def _out_proj_kernel(a_ref, w_ref, h_ref, gate_ref, o_ref):
    a = jnp.concatenate([a_ref[j] for j in range(a_ref.shape[0])], axis=1)
    mix = jnp.dot(a, w_ref[...], preferred_element_type=F32)
    o_ref[...] = h_ref[...] + gate_ref[...] * mix


def _out_proj_residual(a, w, h, gate, tm=512):
    b, s, d = h.shape
    tm = min(tm, s)
    nh = a.shape[1]
    return pl.pallas_call(
        _out_proj_kernel,
        out_shape=jax.ShapeDtypeStruct((b, s, d), F32),
        grid=(b, s // tm),
        in_specs=[
            pl.BlockSpec((None, nh, tm, HEAD_DIM), lambda bi, ti: (bi, 0, ti, 0)),
            pl.BlockSpec(w.shape, lambda bi, ti: (0, 0)),
            pl.BlockSpec((None, tm, d), lambda bi, ti: (bi, ti, 0)),
            pl.BlockSpec((None, 1, d), lambda bi, ti: (bi, 0, 0)),
        ],
        out_specs=pl.BlockSpec((None, tm, d), lambda bi, ti: (bi, ti, 0)),
        compiler_params=_cparams("arbitrary", "arbitrary"),
        name="out_proj_residual",
    )(a, w, h, gate)


def _top16(s, ids=None):
    n = s.shape[0]
    iota = lax.broadcasted_iota(jnp.int32, s.shape, 0).astype(F32)
    vals, picked = [], []
    for _ in range(PEER_TOPK):
        m = jnp.max(s, axis=0, keepdims=True)
        first = jnp.min(jnp.where(s == m, iota, float(n)), axis=0, keepdims=True)
        hit = iota == first
        vals.append(m)
        picked.append(first if ids is None else jnp.sum(jnp.where(hit, ids, 0), axis=0, keepdims=True))
        s = jnp.where(hit, -jnp.inf, s)
    return jnp.concatenate(vals, axis=0), jnp.concatenate(picked, axis=0).astype(jnp.int32)


_PAIR_COUNTS = tuple(PEER_TOPK // (i + 1) for i in range(PEER_TOPK))
_PAIR_PAD = -sum(_PAIR_COUNTS) % 8


def _route_kernel(h_ref, g_ref, sh_ref, sc_ref, wqh_ref, wql_ref, sk_ref, e_ref, gt_ref):
    hn = _rms_mod(h_ref[...], g_ref[...], sh_ref[...], sc_ref[...])
    hn_hi = hn.astype(BF16)
    hn_lo = (hn - hn_hi.astype(F32)).astype(BF16)
    qr = (jnp.dot(hn_hi, wqh_ref[...], preferred_element_type=F32)
          + (jnp.dot(hn_hi, wql_ref[...], preferred_element_type=F32)
             + jnp.dot(hn_lo, wqh_ref[...], preferred_element_type=F32)))
    tm = qr.shape[0]
    for hd in range(PEER_HEADS):
        tops = []
        for p in range(2):
            c0 = (hd * 2 + p) * HEAD_DIM
            sc = lax.dot_general(sk_ref[hd, p], qr[:, c0:c0 + HEAD_DIM], (((1,), (1,)), ((), ())),
                                 precision=HIGHEST, preferred_element_type=F32)
            tops.append(_top16(sc))
        (s1, i1), (s2, i2) = tops
        cand = jnp.concatenate([s1[i:i + 1, :] + s2[:c, :] for i, c in enumerate(_PAIR_COUNTS)]
                               + [jnp.full((_PAIR_PAD, tm), -jnp.inf, F32)], axis=0)
        eid = jnp.concatenate([i1[i:i + 1, :] * PEER_NKEYS + i2[:c, :] for i, c in enumerate(_PAIR_COUNTS)]
                              + [jnp.zeros((_PAIR_PAD, tm), jnp.int32)], axis=0)
        best, experts = _top16(cand, eid)
        ex = jnp.exp(best - best[0:1, :])
        gates = ex / jnp.sum(ex, axis=0, keepdims=True)
        e_ref[hd * PEER_TOPK:(hd + 1) * PEER_TOPK, :] = experts
        gt_ref[hd * PEER_TOPK:(hd + 1) * PEER_TOPK, :] = gates


def _peer_route(h2, g, sh, sc, w_q, subkeys, s, tm=128):
    t, d = h2.shape
    tm = min(tm, s)
    ne = PEER_HEADS * PEER_TOPK
    per_b = s // tm
    row = lambda i: (i // per_b, 0, 0)
    w_hi = w_q.astype(BF16)
    w_lo = (w_q - w_hi.astype(F32)).astype(BF16)
    return pl.pallas_call(
        _route_kernel,
        out_shape=(jax.ShapeDtypeStruct((ne, t), jnp.int32), jax.ShapeDtypeStruct((ne, t), F32)),
        grid=(t // tm,),
        in_specs=[
            pl.BlockSpec((tm, d), lambda i: (i, 0)),
            pl.BlockSpec((1, d), lambda i: (0, 0)),
            pl.BlockSpec((None, 1, d), row),
            pl.BlockSpec((None, 1, d), row),
            pl.BlockSpec(w_q.shape, lambda i: (0, 0)),
            pl.BlockSpec(w_q.shape, lambda i: (0, 0)),
            pl.BlockSpec(subkeys.shape, lambda i: (0, 0, 0, 0)),
        ],
        out_specs=(pl.BlockSpec((ne, tm), lambda i: (0, i)), pl.BlockSpec((ne, tm), lambda i: (0, i))),
        compiler_params=_cparams("arbitrary"),
        name="peer_route",
    )(h2, g, sh, sc, w_hi, w_lo, subkeys)


PEER_RING = 4
PEER_GROUP = 8
PEER_PITCH = 20


def _expert_kernel(idx_ref, idxn_ref, gt_ref, h_ref, g_ref, sh_ref, sc_ref, gate_ref, fg_ref, tab_ref,
                   o_ref, *scratch, tg, ne, final_norm):
    bufs, sem = scratch[:PEER_RING], scratch[PEER_RING]
    i = pl.program_id(0)
    n = pl.num_programs(0)
    d = h_ref.shape[-1]
    rows = tg * ne
    ahead = PEER_RING // 2

    nc = d // HEAD_DIM
    rc = 2 * nc

    def start_token(ids_ref, grp, t, k):
        for e in range(ne):
            src = tab_ref.at[pl.ds(pl.multiple_of(ids_ref[e, grp * tg + t], rc), rc), :]
            pltpu.make_async_copy(src, bufs[k].at[pl.ds((t * ne + e) * PEER_PITCH, rc), :],
                                  sem.at[k]).start(priority=e % 2)

    def wait_group(k):
        pltpu.make_async_copy(tab_ref.at[pl.ds(0, rows * rc)], bufs[k].at[pl.ds(0, rows * rc)], sem.at[k]).wait()

    @pl.when(i == 0)
    def _():
        for k in range(ahead):
            for t in range(tg):
                start_token(idx_ref, k, t, k)

    h = h_ref[...]
    hn = _rms_mod(h, g_ref[...], sh_ref[...], sc_ref[...])
    gt = gt_ref[...]
    outs = []
    for k in range(PEER_RING):
        wait_group(k)
        nk = (k + ahead) % PEER_RING
        ids_ref = idx_ref if k + ahead < PEER_RING else idxn_ref
        for t in range(tg):
            start_token(ids_ref, nk, t, nk)
            tok = k * tg + t

            def chunk(g, c):
                return bufs[k][pl.ds(((t * ne + g * 8) * PEER_PITCH + c), 8, stride=PEER_PITCH), :]

            ws = []
            for g in range(ne // 8):
                acc = chunk(g, 0) * hn[tok:tok + 1, 0:HEAD_DIM]
                for c in range(1, nc):
                    acc = acc + chunk(g, c) * hn[tok:tok + 1, c * HEAD_DIM:(c + 1) * HEAD_DIM]
                hid = jnp.sum(acc, axis=1, keepdims=True)
                ws.append(gt[g * 8:(g + 1) * 8, tok:tok + 1] * _gelu(hid))
            pieces = []
            for c in range(nc):
                acc = ws[0] * chunk(0, nc + c)
                for g in range(1, ne // 8):
                    acc = acc + ws[g] * chunk(g, nc + c)
                pieces.append(jnp.sum(acc, axis=0, keepdims=True))
            outs.append(jnp.concatenate(pieces, axis=1))
    y = h + gate_ref[...] * jnp.concatenate(outs, axis=0)
    if final_norm:
        y = y * lax.rsqrt(jnp.mean(y * y, axis=-1, keepdims=True) + NORM_EPS) * fg_ref[...]
    o_ref[...] = y

    @pl.when(i == n - 1)
    def _():
        for k in range(ahead):
            wait_group(k)


def _peer_experts(h2, experts, gates, g, sh, sc, gate, final_g, table, s, final_norm):
    t, d = h2.shape
    ne = experts.shape[0]
    tg = PEER_GROUP
    tb = PEER_RING * tg
    nblk = t // tb
    per_b = s // tb
    row = lambda i: (i // per_b, 0, 0)
    rc = 2 * d // HEAD_DIM
    experts = (experts * rc).reshape(ne, nblk, tb).transpose(1, 0, 2)
    gates = gates.reshape(ne, nblk, tb).transpose(1, 0, 2)
    return pl.pallas_call(
        functools.partial(_expert_kernel, tg=tg, ne=ne, final_norm=final_norm),
        out_shape=jax.ShapeDtypeStruct((t, d), F32),
        grid=(nblk,),
        in_specs=[
            pl.BlockSpec((None, ne, tb), lambda i: (i, 0, 0), memory_space=pltpu.SMEM),
            pl.BlockSpec((None, ne, tb), lambda i: (jnp.minimum(i + 1, nblk - 1), 0, 0), memory_space=pltpu.SMEM),
            pl.BlockSpec((None, ne, tb), lambda i: (i, 0, 0)),
            pl.BlockSpec((tb, d), lambda i: (i, 0)),
            pl.BlockSpec((1, d), lambda i: (0, 0)),
            pl.BlockSpec((None, 1, d), row),
            pl.BlockSpec((None, 1, d), row),
            pl.BlockSpec((None, 1, d), row),
            pl.BlockSpec((1, d), lambda i: (0, 0)),
            pl.BlockSpec(memory_space=pl.ANY),
        ],
        out_specs=pl.BlockSpec((tb, d), lambda i: (i, 0)),
        scratch_shapes=([pltpu.VMEM((tg * ne * PEER_PITCH, HEAD_DIM), F32)] * PEER_RING
                        + [pltpu.SemaphoreType.DMA((PEER_RING,))]),
        compiler_params=_cparams("arbitrary"),
        name="peer_experts",
    )(experts, experts, gates, h2, g, sh, sc, gate, final_g, table)


def kernel(x, c, ada_w, ada_b, norm_mix_g, norm_ffn_g, hgrn_w_in, hgrn_lb_logits, hgrn_onorm_g, hgrn_w_out,
           kv_ada_w, kv_ada_b, kv_norm_g, kv_w, sb_w_q, sb_w_out, peer_w_q, peer_subkeys, peer_u, peer_v,
           final_norm_g):
    b, s, d = x.shape
    depth = ada_w.shape[0]
    n_a = hgrn_w_in.shape[0]
    c8 = jnp.zeros((8, d), F32).at[:b].set(c)

    def mods(w3, b3, layer, parts):
        m = _modulation(c8, w3, b3.reshape(b3.shape[0], 1, -1), layer)[:b]
        return [p.reshape(b, 1, d) for p in jnp.split(m, parts, axis=-1)]

    n_exp = peer_u.shape[1]
    table = jnp.concatenate([peer_u.reshape(depth * n_exp, -1, HEAD_DIM), peer_v.reshape(depth * n_exp, -1, HEAD_DIM)],
                            axis=1).reshape(-1, HEAD_DIM)
    h = x
    kv = None
    for l in range(depth):
        sh1, sc1, g1, sh2, sc2, g2 = mods(ada_w, ada_b, l, 6)
        gm = norm_mix_g[l].reshape(1, d)
        if l < n_a:
            h = _hgrn_layer(h, gm, sh1, sc1, g1, hgrn_w_in[l].astype(BF16), hgrn_lb_logits,
                            hgrn_onorm_g[l].reshape(1, d), hgrn_w_out[l].astype(BF16), l)
        else:
            q = _proj_heads(h, gm, sh1, sc1, sb_w_q[l - n_a].astype(BF16),
                            out_scale=1.0 / math.sqrt(HEAD_DIM))
            a = _sb_attention(q, kv)
            h = _out_proj_residual(a, sb_w_out[l - n_a].astype(BF16), h, g1)
        gf = norm_ffn_g[l].reshape(1, d)
        h2 = h.reshape(b * s, d)
        experts, gates = _peer_route(h2, gf, sh2, sc2, peer_w_q[l], peer_subkeys[l], s)
        h2 = _peer_experts(h2, experts + l * n_exp, gates, gf, sh2, sc2, g2, final_norm_g.reshape(1, d), table, s,
                           final_norm=(l == depth - 1))
        h = h2.reshape(b, s, d)
        if l == n_a - 1:
            ksh, ksc = mods(kv_ada_w.reshape(1, d, 2 * d), kv_ada_b.reshape(1, 2 * d), 0, 2)
            kv = _proj_heads(h, kv_norm_g.reshape(1, d), ksh, ksc, kv_w.astype(BF16))
    return h
```

```python
import functools
import math

import jax
import jax.numpy as jnp
from jax import lax
from jax.experimental import pallas as pl
from jax.experimental.pallas import tpu as pltpu

F32 = jnp.float32
BF16 = jnp.bfloat16
HIGHEST = lax.Precision.HIGHEST

NORM_EPS = 1e-6
HEAD_DIM = 128
HGRN_CHUNK = 32
SB_BLOCK = 128
PEER_HEADS = 4
PEER_NKEYS = 128
PEER_TOPK = 16
SB_LOG_FLOOR = -100.0

VMEM_LIMIT = 56 * 1024 * 1024


def _cparams(*sem):
    return pltpu.CompilerParams(dimension_semantics=sem, vmem_limit_bytes=VMEM_LIMIT)


def _rms_mod(h, g, shift, scale):
    ms = jnp.mean(h * h, axis=-1, keepdims=True)
    return (h * lax.rsqrt(ms + NORM_EPS) * g) * (1.0 + scale) + shift


def _gelu(x):
    return 0.5 * x * (1.0 + lax.erf(x * (1.0 / math.sqrt(2.0))))


def _mod_kernel(c_ref, w_ref, b_ref, o_ref):
    c = c_ref[...]
    ca = c * jax.nn.sigmoid(c)
    o_ref[...] = jnp.dot(ca, w_ref[...], precision=HIGHEST, preferred_element_type=F32) + b_ref[...]


def _modulation(c8, w3, b3, layer, tn=1024):
    _, d, n = w3.shape
    return pl.pallas_call(
        _mod_kernel,
        out_shape=jax.ShapeDtypeStruct((8, n), F32),
        grid=(n // tn,),
        in_specs=[
            pl.BlockSpec((8, d), lambda j: (0, 0)),
            pl.BlockSpec((None, d, tn), lambda j: (layer, 0, j)),
            pl.BlockSpec((None, 1, tn), lambda j: (layer, 0, j)),
        ],
        out_specs=pl.BlockSpec((8, tn), lambda j: (0, j)),
        compiler_params=_cparams("arbitrary"),
        name="adaln_modulation",
    )(c8, w3, b3)


def _hgrn_kernel(h_ref, g_ref, sh_ref, sc_ref, gate_ref, win_ref, lbl_ref, og_ref, wout_ref,
                 o_ref, q_s, k_s, lf_s, v_s, gg_s, ob_s, st_s, *, layer, ts, nh):
    d = h_ref.shape[-1]
    c = HGRN_CHUNK

    @pl.when(pl.program_id(1) == 0)
    def _():
        st_s[...] = jnp.zeros_like(st_s)

    h = h_ref[...]
    hn = _rms_mod(h, g_ref[...], sh_ref[...], sc_ref[...])
    proj = jnp.dot(hn.astype(BF16), win_ref[...], preferred_element_type=F32)
    lbl = lbl_ref[...]
    e = jnp.exp(lbl - jnp.max(lbl, axis=0, keepdims=True))
    lb = jnp.sum(e[: layer + 1], axis=0, keepdims=True) / jnp.sum(e, axis=0, keepdims=True)
    fg = lb + (1.0 - lb) * jax.nn.sigmoid(proj[:, d:2 * d])
    q = proj[:, :d]
    q_s[...] = q * jax.nn.sigmoid(q)
    k_s[...] = 1.0 - fg
    lf_s[...] = jnp.log(fg)
    v_s[...] = proj[:, 2 * d:3 * d]
    gg = proj[:, 3 * d:]
    gg_s[...] = gg * jax.nn.sigmoid(gg)

    row = lax.broadcasted_iota(jnp.int32, (c, c), 0)
    col = lax.broadcasted_iota(jnp.int32, (c, c), 1)
    causal = row >= col
    tril = causal.astype(F32)

    def chunk(ci, carry):
        r0 = pl.multiple_of(ci * c, c)
        lf = lf_s[pl.ds(r0, c), :]
        qc = q_s[pl.ds(r0, c), :]
        kc = k_s[pl.ds(r0, c), :]
        vc = v_s[pl.ds(r0, c), :].astype(BF16)
        cum = jnp.dot(tril, lf, precision=HIGHEST, preferred_element_type=F32)
        mid = cum[c // 2 - 1:c // 2, :]
        last = cum[c - 1:c, :]
        qh = (qc * jnp.exp(cum - mid)).astype(BF16)
        kh = (kc * jnp.exp(mid - cum)).astype(BF16)
        qs = (qc * jnp.exp(cum)).astype(BF16)
        kl = (kc * jnp.exp(last - cum)).astype(BF16)
        dec = jnp.exp(last)
        for hd in range(nh):
            sl = slice(hd * HEAD_DIM, (hd + 1) * HEAD_DIM)
            sc = lax.dot_general(qh[:, sl], kh[:, sl], (((1,), (1,)), ((), ())), preferred_element_type=F32)
            sc = jnp.where(causal, sc, 0.0).astype(BF16)
            st = st_s[hd]
            o = jnp.dot(sc, vc[:, sl], preferred_element_type=F32)
            o = o + lax.dot_general(qs[:, sl], st.astype(BF16), (((1,), (1,)), ((), ())),
                                    preferred_element_type=F32)
            ob_s[pl.ds(r0, c), sl] = o
            st_s[hd] = st * dec[:, sl] + lax.dot_general(vc[:, sl], kl[:, sl], (((0,), (0,)), ((), ())),
                                                         preferred_element_type=F32)
        return carry

    lax.fori_loop(0, ts // c, chunk, 0, unroll=2)

    og = og_ref[...]
    gg = gg_s[...]
    parts = []
    for hd in range(nh):
        sl = slice(hd * HEAD_DIM, (hd + 1) * HEAD_DIM)
        o = ob_s[:, sl]
        o = o * lax.rsqrt(jnp.mean(o * o, axis=-1, keepdims=True) + NORM_EPS)
        parts.append((o * og[:, sl] * gg[:, sl]).astype(BF16))
    on = jnp.concatenate(parts, axis=1)
    mix = jnp.dot(on, wout_ref[...], preferred_element_type=F32)
    o_ref[...] = h + gate_ref[...] * mix


def _hgrn_layer(h, g, sh, sc, gate, w_in, lb_logits, onorm_g, w_out, layer, ts=256):
    b, s, d = h.shape
    nh = d // HEAD_DIM
    row = lambda bi, ti: (bi, 0, 0)
    const2 = lambda bi, ti: (0, 0)
    return pl.pallas_call(
        functools.partial(_hgrn_kernel, layer=layer, ts=ts, nh=nh),
        out_shape=jax.ShapeDtypeStruct((b, s, d), F32),
        grid=(b, s // ts),
        in_specs=[
            pl.BlockSpec((None, ts, d), lambda bi, ti: (bi, ti, 0)),
            pl.BlockSpec((1, d), const2),
            pl.BlockSpec((None, 1, d), row),
            pl.BlockSpec((None, 1, d), row),
            pl.BlockSpec((None, 1, d), row),
            pl.BlockSpec(w_in.shape, const2),
            pl.BlockSpec(lb_logits.shape, const2),
            pl.BlockSpec((1, d), const2),
            pl.BlockSpec(w_out.shape, const2),
        ],
        out_specs=pl.BlockSpec((None, ts, d), lambda bi, ti: (bi, ti, 0)),
        scratch_shapes=[pltpu.VMEM((ts, d), F32)] * 6 + [pltpu.VMEM((nh, HEAD_DIM, HEAD_DIM), F32)],
        compiler_params=_cparams("arbitrary", "arbitrary"),
        name="hgrn2_layer",
    )(h, g, sh, sc, gate, w_in, lb_logits, onorm_g, w_out)


def _proj_heads_kernel(h_ref, g_ref, sh_ref, sc_ref, w_ref, o_ref, *, out_scale):
    hn = _rms_mod(h_ref[...], g_ref[...], sh_ref[...], sc_ref[...])
    y = jnp.dot(hn.astype(BF16), w_ref[...], preferred_element_type=F32)
    if out_scale != 1.0:
        y = y * out_scale
    for j in range(o_ref.shape[0]):
        o_ref[j] = y[:, j * HEAD_DIM:(j + 1) * HEAD_DIM].astype(o_ref.dtype)


def _proj_heads(h, g, sh, sc, w, out_scale=1.0, tm=512):
    b, s, d = h.shape
    tm = min(tm, s)
    n = w.shape[1]
    nj = n // HEAD_DIM
    row = lambda bi, ti: (bi, 0, 0)
    const2 = lambda bi, ti: (0, 0)
    return pl.pallas_call(
        functools.partial(_proj_heads_kernel, out_scale=out_scale),
        out_shape=jax.ShapeDtypeStruct((b, nj, s, HEAD_DIM), BF16),
        grid=(b, s // tm),
        in_specs=[
            pl.BlockSpec((None, tm, d), lambda bi, ti: (bi, ti, 0)),
            pl.BlockSpec((1, d), const2),
            pl.BlockSpec((None, 1, d), row),
            pl.BlockSpec((None, 1, d), row),
            pl.BlockSpec(w.shape, const2),
        ],
        out_specs=pl.BlockSpec((None, nj, tm, HEAD_DIM), lambda bi, ti: (bi, 0, ti, 0)),
        compiler_params=_cparams("arbitrary", "arbitrary"),
        name="norm_mod_proj",
    )(h, g, sh, sc, w)


SB_HEADS_PER_STEP = 2
SB_QTILE = 2048
SB_AHEAD = 2


def _sb_attn_kernel(q_ref, k_ref, v_ref, o_ref, kn_ref, *, s, tq):
    qb = SB_BLOCK
    nh = q_ref.shape[0]
    row = lax.broadcasted_iota(jnp.int32, (qb, qb), 0)
    col = lax.broadcasted_iota(jnp.int32, (qb, qb), 1)
    strict = col < row
    tri = (row >= col).astype(BF16)
    ones = jnp.ones((qb, qb), BF16)

    @pl.when(pl.program_id(2) == 0)
    def _():
        rows = min(s, 8 * qb)
        for hd in range(nh):
            def kmax_body(j, m, hd=hd):
                kb = k_ref[hd, pl.ds(pl.multiple_of(j * rows, rows), rows), :].astype(F32)
                ssq = jnp.dot((kb * kb).astype(BF16), ones, preferred_element_type=F32)
                return jnp.maximum(m, jnp.max(ssq, axis=0, keepdims=True))
            m = lax.fori_loop(0, s // rows, kmax_body, jnp.zeros((1, qb), F32))
            kn_ref[hd] = jnp.sqrt(jnp.max(m)) * 1.01

    def scores(hd, q, j, keep):
        r0 = pl.multiple_of(j * qb, qb)
        kb = k_ref[hd, pl.ds(r0, qb), :]
        vb = v_ref[hd, pl.ds(r0, qb), :]
        z = lax.dot_general(q, kb, (((1,), (1,)), ((), ())), preferred_element_type=F32)
        lk = jax.nn.log_sigmoid(-z)
        if keep is not None:
            lk = jnp.where(keep, lk, 0.0)
        hi = lk.astype(BF16)
        lo = (lk - hi.astype(F32)).astype(BF16)
        cs = (jnp.dot(hi, tri, preferred_element_type=F32)
              + jnp.dot(lo, tri, preferred_element_type=F32))
        return z, cs, vb

    def accumulate(z, cs, vb, keep, after, acc):
        a = jnp.exp(z + cs + after)
        if keep is not None:
            a = jnp.where(keep, a, 0.0)
        acc = acc + jnp.dot(a.astype(BF16), vb, preferred_element_type=F32)
        return after + cs[:, 0:1], acc

    def block(hd, q, j, after, acc):
        z, cs, vb = scores(hd, q, j, None)
        return accumulate(z, cs, vb, None, after, acc)

    def qblock(i, carry):
        q0 = pl.multiple_of(i * qb, qb)
        gi = pl.program_id(2) * (tq // qb) + i
        qs, zbounds, afters, accs = [], [], [], []
        for hd in range(nh):
            q = q_ref[hd, pl.ds(q0, qb), :]
            qf = q.astype(F32)
            zbounds.append(jnp.sqrt(jnp.sum(qf * qf, axis=1, keepdims=True)) * kn_ref[hd])
            qs.append(q)
        for hd in range(nh):
            keeps = [strict] + [gi - a >= 0 for a in range(1, SB_AHEAD + 1)]
            parts = [scores(hd, qs[hd], jnp.maximum(gi - a, 0), keeps[a]) for a in range(SB_AHEAD + 1)]
            after, acc = jnp.zeros((qb, 1), F32), jnp.zeros((qb, HEAD_DIM), F32)
            for (z, cs, vb), keep in zip(parts, keeps):
                after, acc = accumulate(z, cs, vb, keep, after, acc)
            afters.append(after)
            accs.append(acc)

        def cond(st):
            j, afters, _ = st
            live = jnp.max(zbounds[0] + afters[0])
            for hd in range(1, nh):
                live = jnp.maximum(live, jnp.max(zbounds[hd] + afters[hd]))
            return jnp.logical_and(j >= 0, live > SB_LOG_FLOOR)

        def body(st):
            j, afters, accs = st
            out = [block(hd, qs[hd], j, afters[hd], accs[hd]) for hd in range(nh)]
            return j - 1, tuple(o[0] for o in out), tuple(o[1] for o in out)

        _, _, accs = lax.while_loop(cond, body, (gi - SB_AHEAD - 1, tuple(afters), tuple(accs)))
        for hd in range(nh):
            o_ref[hd, pl.ds(q0, qb), :] = accs[hd].astype(o_ref.dtype)
        return carry

    lax.fori_loop(0, tq // qb, qblock, 0)


def _sb_attention(q, kv):
    b, nh, s, dh = q.shape
    hp = SB_HEADS_PER_STEP
    tq = min(SB_QTILE, s)
    qblk = (None, hp, tq, dh)
    kblk = (None, hp, s, dh)
    return pl.pallas_call(
        functools.partial(_sb_attn_kernel, s=s, tq=tq),
        out_shape=jax.ShapeDtypeStruct((b, nh, s, dh), BF16),
        grid=(b, nh // hp, s // tq),
        in_specs=[
            pl.BlockSpec(qblk, lambda bi, hi, qi: (bi, hi, qi, 0)),
            pl.BlockSpec(kblk, lambda bi, hi, qi: (bi, hi, 0, 0)),
            pl.BlockSpec(kblk, lambda bi, hi, qi: (bi, nh // hp + hi, 0, 0)),
        ],
        out_specs=pl.BlockSpec(qblk, lambda bi, hi, qi: (bi, hi, qi, 0)),
        scratch_shapes=[pltpu.SMEM((hp,), F32)],
        compiler_params=_cparams("arbitrary", "arbitrary", "arbitrary"),
        name="stick_breaking_attention",
    )(q, kv, kv)


def _out_proj_kernel(a_ref, w_ref, h_ref, gate_ref, o_ref):
    a = jnp.concatenate([a_ref[j] for j in range(a_ref.shape[0])], axis=1)
    mix = jnp.dot(a, w_ref[...], preferred_element_type=F32)
    o_ref[...] = h_ref[...] + gate_ref[...] * mix


def _out_proj_residual(a, w, h, gate, tm=512):
    b, s, d = h.shape
    tm = min(tm, s)
    nh = a.shape[1]
    return pl.pallas_call(
        _out_proj_kernel,
        out_shape=jax.ShapeDtypeStruct((b, s, d), F32),
        grid=(b, s // tm),
        in_specs=[
            pl.BlockSpec((None, nh, tm, HEAD_DIM), lambda bi, ti: (bi, 0, ti, 0)),
            pl.BlockSpec(w.shape, lambda bi, ti: (0, 0)),
            pl.BlockSpec((None, tm, d), lambda bi, ti: (bi, ti, 0)),
            pl.BlockSpec((None, 1, d), lambda bi, ti: (bi, 0, 0)),
        ],
        out_specs=pl.BlockSpec((None, tm, d), lambda bi, ti: (bi, ti, 0)),
        compiler_params=_cparams("arbitrary", "arbitrary"),
        name="out_proj_residual",
    )(a, w, h, gate)


PEER_RING = 4
PEER_GROUP = 8
PEER_ROWS = 8
PEER_PITCH = 10

def _top16(s, ids=None):
    n = s.shape[0]
    iota = lax.broadcasted_iota(jnp.int32, s.shape, 0).astype(F32)
    vals, picked = [], []
    for _ in range(PEER_TOPK):
        m = jnp.max(s, axis=0, keepdims=True)
        first = jnp.min(jnp.where(s == m, iota, float(n)), axis=0, keepdims=True)
        hit = iota == first
        vals.append(m)
        picked.append(first if ids is None else jnp.sum(jnp.where(hit, ids, 0), axis=0, keepdims=True))
        s = jnp.where(hit, -jnp.inf, s)
    return jnp.concatenate(vals, axis=0), jnp.concatenate(picked, axis=0).astype(jnp.int32)


_PAIR_COUNTS = tuple(PEER_TOPK // (i + 1) for i in range(PEER_TOPK))
_PAIR_PAD = -sum(_PAIR_COUNTS) % 8


def _route_kernel(h_ref, g_ref, sh_ref, sc_ref, wqh_ref, wql_ref, sk_ref, e_ref, gt_ref, *, row_base, tb):
    hn = _rms_mod(h_ref[...], g_ref[...], sh_ref[...], sc_ref[...])
    hn_hi = hn.astype(BF16)
    hn_lo = (hn - hn_hi.astype(F32)).astype(BF16)
    qr = (jnp.dot(hn_hi, wqh_ref[...], preferred_element_type=F32)
          + (jnp.dot(hn_hi, wql_ref[...], preferred_element_type=F32)
             + jnp.dot(hn_lo, wqh_ref[...], preferred_element_type=F32)))
    tm = qr.shape[0]
    for hd in range(PEER_HEADS):
        tops = []
        for p in range(2):
            c0 = (hd * 2 + p) * HEAD_DIM
            sc = lax.dot_general(sk_ref[hd, p], qr[:, c0:c0 + HEAD_DIM], (((1,), (1,)), ((), ())),
                                 precision=HIGHEST, preferred_element_type=F32)
            tops.append(_top16(sc))
        (s1, i1), (s2, i2) = tops
        cand = jnp.concatenate([s1[i:i + 1, :] + s2[:c, :] for i, c in enumerate(_PAIR_COUNTS)]
                               + [jnp.full((_PAIR_PAD, tm), -jnp.inf, F32)], axis=0)
        eid = jnp.concatenate([i1[i:i + 1, :] * PEER_NKEYS + i2[:c, :] for i, c in enumerate(_PAIR_COUNTS)]
                              + [jnp.zeros((_PAIR_PAD, tm), jnp.int32)], axis=0)
        best, experts = _top16(cand, eid)
        ex = jnp.exp(best - best[0:1, :])
        gates = ex / jnp.sum(ex, axis=0, keepdims=True)
        first_row = (experts + row_base) * PEER_ROWS
        for j in range(tm // tb):
            e_ref[j, hd * PEER_TOPK:(hd + 1) * PEER_TOPK, :] = first_row[:, j * tb:(j + 1) * tb]
            gt_ref[j, hd * PEER_TOPK:(hd + 1) * PEER_TOPK, :] = gates[:, j * tb:(j + 1) * tb]


def _peer_route(h2, g, sh, sc, w_q, subkeys, s, row_base, tm=128):
    t, d = h2.shape
    tm = min(tm, s)
    tb = PEER_RING * PEER_GROUP
    ne = PEER_HEADS * PEER_TOPK
    per_b = s // tm
    row = lambda i: (i // per_b, 0, 0)
    w_hi = w_q.astype(BF16)
    w_lo = (w_q - w_hi.astype(F32)).astype(BF16)
    return pl.pallas_call(
        functools.partial(_route_kernel, row_base=row_base, tb=tb),
        out_shape=(jax.ShapeDtypeStruct((t // tb, ne, tb), jnp.int32), jax.ShapeDtypeStruct((t // tb, ne, tb), F32)),
        grid=(t // tm,),
        in_specs=[
            pl.BlockSpec((tm, d), lambda i: (i, 0)),
            pl.BlockSpec((1, d), lambda i: (0, 0)),
            pl.BlockSpec((None, 1, d), row),
            pl.BlockSpec((None, 1, d), row),
            pl.BlockSpec(w_q.shape, lambda i: (0, 0)),
            pl.BlockSpec(w_q.shape, lambda i: (0, 0)),
            pl.BlockSpec(subkeys.shape, lambda i: (0, 0, 0, 0)),
        ],
        out_specs=(pl.BlockSpec((tm // tb, ne, tb), lambda i: (i, 0, 0)),
                   pl.BlockSpec((tm // tb, ne, tb), lambda i: (i, 0, 0))),
        compiler_params=_cparams("arbitrary"),
        name="peer_route",
    )(h2, g, sh, sc, w_hi, w_lo, subkeys)


def _pack_kernel(u_ref, v_ref, o_ref):
    tm, d = u_ref.shape
    nr = d // (2 * HEAD_DIM)
    for src, base in ((u_ref, 0), (v_ref, nr)):
        bits = lax.bitcast_convert_type(src[...].astype(BF16).astype(F32), jnp.uint32)
        for r in range(nr):
            word = ((bits[:, (2 * r) * HEAD_DIM:(2 * r + 1) * HEAD_DIM] >> 16)
                    | bits[:, (2 * r + 1) * HEAD_DIM:(2 * r + 2) * HEAD_DIM])
            o_ref[pl.ds(base + r, tm, stride=PEER_ROWS), :] = word


def _pack_experts(u, v, tm=512):
    n, d = u.shape
    assert d == PEER_ROWS * HEAD_DIM
    return pl.pallas_call(
        _pack_kernel,
        out_shape=jax.ShapeDtypeStruct((n * PEER_ROWS, HEAD_DIM), jnp.uint32),
        grid=(n // tm,),
        in_specs=[pl.BlockSpec((tm, d), lambda i: (i, 0)), pl.BlockSpec((tm, d), lambda i: (i, 0))],
        out_specs=pl.BlockSpec((tm * PEER_ROWS, HEAD_DIM), lambda i: (i, 0)),
        compiler_params=_cparams("arbitrary"),
        name="peer_pack_experts",
    )(u, v)


def _unpack_pair(w):
    lo = lax.bitcast_convert_type(w << 16, F32)
    hi = lax.bitcast_convert_type(w & jnp.uint32(0xFFFF0000), F32)
    return lo, hi


def _expert_kernel(idx_ref, idxn_ref, gt_ref, h_ref, g_ref, sh_ref, sc_ref, gate_ref, fg_ref, tab_ref,
                   o_ref, *scratch, tg, ne, final_norm):
    bufs, sem = scratch[:PEER_RING], scratch[PEER_RING]
    i = pl.program_id(0)
    n = pl.num_programs(0)
    d = h_ref.shape[-1]
    rows = tg * ne
    ahead = PEER_RING // 2

    nr = d // (2 * HEAD_DIM)
    rc = PEER_ROWS

    def start_token(ids_ref, grp, t, k):
        for e in range(ne):
            src = tab_ref.at[pl.ds(pl.multiple_of(ids_ref[e, grp * tg + t], rc), rc), :]
            pltpu.make_async_copy(src, bufs[k].at[pl.ds((t * ne + e) * PEER_PITCH, rc), :],
                                  sem.at[k]).start(priority=e % 2)

    def wait_group(k):
        pltpu.make_async_copy(tab_ref.at[pl.ds(0, rows * rc)], bufs[k].at[pl.ds(0, rows * rc)], sem.at[k]).wait()

    @pl.when(i == 0)
    def _():
        for k in range(ahead):
            for t in range(tg):
                start_token(idx_ref, k, t, k)

    h = h_ref[...]
    hn = _rms_mod(h, g_ref[...], sh_ref[...], sc_ref[...])
    gt = gt_ref[...]
    outs = []
    for k in range(PEER_RING):
        wait_group(k)
        nk = (k + ahead) % PEER_RING
        ids_ref = idx_ref if k + ahead < PEER_RING else idxn_ref
        for t in range(tg):
            start_token(ids_ref, nk, t, nk)
            tok = k * tg + t

            def pair(g, r):
                return _unpack_pair(bufs[k][pl.ds(((t * ne + g * 8) * PEER_PITCH + r), 8, stride=PEER_PITCH), :])

            def xchunk(c):
                return hn[tok:tok + 1, c * HEAD_DIM:(c + 1) * HEAD_DIM]

            ws = []
            for g in range(ne // 8):
                acc = None
                for r in range(nr):
                    lo, hi = pair(g, r)
                    part = lo * xchunk(2 * r) + hi * xchunk(2 * r + 1)
                    acc = part if acc is None else acc + part
                hid = jnp.sum(acc, axis=1, keepdims=True)
                ws.append(gt[g * 8:(g + 1) * 8, tok:tok + 1] * _gelu(hid))
            pieces = []
            for r in range(nr):
                acc_lo = acc_hi = None
                for g in range(ne // 8):
                    lo, hi = pair(g, nr + r)
                    acc_lo = ws[g] * lo if acc_lo is None else acc_lo + ws[g] * lo
                    acc_hi = ws[g] * hi if acc_hi is None else acc_hi + ws[g] * hi
                pieces.append(jnp.sum(acc_lo, axis=0, keepdims=True))
                pieces.append(jnp.sum(acc_hi, axis=0, keepdims=True))
            outs.append(jnp.concatenate(pieces, axis=1))
    y = h + gate_ref[...] * jnp.concatenate(outs, axis=0)
    if final_norm:
        y = y * lax.rsqrt(jnp.mean(y * y, axis=-1, keepdims=True) + NORM_EPS) * fg_ref[...]
    o_ref[...] = y

    @pl.when(i == n - 1)
    def _():
        for k in range(ahead):
            wait_group(k)


def _peer_experts(h2, experts, gates, g, sh, sc, gate, final_g, table, s, final_norm):
    t, d = h2.shape
    assert d == PEER_ROWS * HEAD_DIM
    nblk, ne, tb = experts.shape
    tg = PEER_GROUP
    per_b = s // tb
    row = lambda i: (i // per_b, 0, 0)
    return pl.pallas_call(
        functools.partial(_expert_kernel, tg=tg, ne=ne, final_norm=final_norm),
        out_shape=jax.ShapeDtypeStruct((t, d), F32),
        grid=(nblk,),
        in_specs=[
            pl.BlockSpec((None, ne, tb), lambda i: (i, 0, 0), memory_space=pltpu.SMEM),
            pl.BlockSpec((None, ne, tb), lambda i: (jnp.minimum(i + 1, nblk - 1), 0, 0), memory_space=pltpu.SMEM),
            pl.BlockSpec((None, ne, tb), lambda i: (i, 0, 0)),
            pl.BlockSpec((tb, d), lambda i: (i, 0)),
            pl.BlockSpec((1, d), lambda i: (0, 0)),
            pl.BlockSpec((None, 1, d), row),
            pl.BlockSpec((None, 1, d), row),
            pl.BlockSpec((None, 1, d), row),
            pl.BlockSpec((1, d), lambda i: (0, 0)),
            pl.BlockSpec(memory_space=pl.ANY),
        ],
        out_specs=pl.BlockSpec((tb, d), lambda i: (i, 0)),
        scratch_shapes=([pltpu.VMEM((tg * ne * PEER_PITCH, HEAD_DIM), jnp.uint32)] * PEER_RING
                        + [pltpu.SemaphoreType.DMA((PEER_RING,))]),
        compiler_params=_cparams("arbitrary"),
        name="peer_experts",
    )(experts, experts, gates, h2, g, sh, sc, gate, final_g, table)


def kernel(x, c, ada_w, ada_b, norm_mix_g, norm_ffn_g, hgrn_w_in, hgrn_lb_logits, hgrn_onorm_g, hgrn_w_out,
           kv_ada_w, kv_ada_b, kv_norm_g, kv_w, sb_w_q, sb_w_out, peer_w_q, peer_subkeys, peer_u, peer_v,
           final_norm_g):
    b, s, d = x.shape
    depth = ada_w.shape[0]
    n_a = hgrn_w_in.shape[0]
    c8 = jnp.zeros((8, d), F32).at[:b].set(c)

    def mods(w3, b3, layer, parts):
        m = _modulation(c8, w3, b3.reshape(b3.shape[0], 1, -1), layer)[:b]
        return [p.reshape(b, 1, d) for p in jnp.split(m, parts, axis=-1)]

    n_exp = peer_u.shape[1]

    table = _pack_experts(peer_u.reshape(depth * n_exp, d), peer_v.reshape(depth * n_exp, d))
    h = x
    kv = None
    for l in range(depth):
        sh1, sc1, g1, sh2, sc2, g2 = mods(ada_w, ada_b, l, 6)
        gm = norm_mix_g[l].reshape(1, d)
        if l < n_a:
            h = _hgrn_layer(h, gm, sh1, sc1, g1, hgrn_w_in[l].astype(BF16), hgrn_lb_logits,
                            hgrn_onorm_g[l].reshape(1, d), hgrn_w_out[l].astype(BF16), l)
        else:
            q = _proj_heads(h, gm, sh1, sc1, sb_w_q[l - n_a].astype(BF16),
                            out_scale=1.0 / math.sqrt(HEAD_DIM))
            a = _sb_attention(q, kv)
            h = _out_proj_residual(a, sb_w_out[l - n_a].astype(BF16), h, g1)
        gf = norm_ffn_g[l].reshape(1, d)
        h2 = h.reshape(b * s, d)
        experts, gates = _peer_route(h2, gf, sh2, sc2, peer_w_q[l], peer_subkeys[l], s, row_base=l * n_exp)
        h2 = _peer_experts(h2, experts, gates, gf, sh2, sc2, g2, final_norm_g.reshape(1, d), table, s,
                           final_norm=(l == depth - 1))
        h = h2.reshape(b, s, d)
        if l == n_a - 1:
            ksh, ksc = mods(kv_ada_w.reshape(1, d, 2 * d), kv_ada_b.reshape(1, 2 * d), 0, 2)
            kv = _proj_heads(h, kv_norm_g.reshape(1, d), ksh, ksc, kv_w.astype(BF16))
    return h
```

```python
import functools
import math

import jax
import jax.numpy as jnp
from jax import lax
from jax.experimental import pallas as pl
from jax.experimental.pallas import tpu as pltpu

F32 = jnp.float32
BF16 = jnp.bfloat16
HIGHEST = lax.Precision.HIGHEST

NORM_EPS = 1e-6
HEAD_DIM = 128
HGRN_CHUNK = 32
SB_BLOCK = 128
PEER_HEADS = 4
PEER_NKEYS = 128
PEER_TOPK = 16
SB_LOG_FLOOR = -100.0

VMEM_LIMIT = 56 * 1024 * 1024


def _cparams(*sem):
    return pltpu.CompilerParams(dimension_semantics=sem, vmem_limit_bytes=VMEM_LIMIT)


def _rms_mod(h, g, shift, scale):
    ms = jnp.mean(h * h, axis=-1, keepdims=True)
    return (h * lax.rsqrt(ms + NORM_EPS) * g) * (1.0 + scale) + shift


def _gelu(x):
    return 0.5 * x * (1.0 + lax.erf(x * (1.0 / math.sqrt(2.0))))


def _mod_kernel(c_ref, w_ref, b_ref, o_ref):
    c = c_ref[...]
    ca = c * jax.nn.sigmoid(c)
    o_ref[...] = jnp.dot(ca, w_ref[...], precision=HIGHEST, preferred_element_type=F32) + b_ref[...]


def _modulation(c8, w3, b3, layer, tn=1024):
    _, d, n = w3.shape
    return pl.pallas_call(
        _mod_kernel,
        out_shape=jax.ShapeDtypeStruct((8, n), F32),
        grid=(n // tn,),
        in_specs=[
            pl.BlockSpec((8, d), lambda j: (0, 0)),
            pl.BlockSpec((None, d, tn), lambda j: (layer, 0, j)),
            pl.BlockSpec((None, 1, tn), lambda j: (layer, 0, j)),
        ],
        out_specs=pl.BlockSpec((8, tn), lambda j: (0, j)),
        compiler_params=_cparams("arbitrary"),
        name="adaln_modulation",
    )(c8, w3, b3)


def _hgrn_kernel(h_ref, g_ref, sh_ref, sc_ref, gate_ref, win_ref, lbl_ref, og_ref, wout_ref,
                 o_ref, q_s, k_s, lf_s, v_s, gg_s, ob_s, st_s, *, layer, ts, nh):
    d = h_ref.shape[-1]
    c = HGRN_CHUNK

    @pl.when(pl.program_id(1) == 0)
    def _():
        st_s[...] = jnp.zeros_like(st_s)

    h = h_ref[...]
    hn = _rms_mod(h, g_ref[...], sh_ref[...], sc_ref[...])
    proj = jnp.dot(hn.astype(BF16), win_ref[...], preferred_element_type=F32)
    lbl = lbl_ref[...]
    e = jnp.exp(lbl - jnp.max(lbl, axis=0, keepdims=True))
    lb = jnp.sum(e[: layer + 1], axis=0, keepdims=True) / jnp.sum(e, axis=0, keepdims=True)
    fg = lb + (1.0 - lb) * jax.nn.sigmoid(proj[:, d:2 * d])
    q = proj[:, :d]
    q_s[...] = q * jax.nn.sigmoid(q)
    k_s[...] = 1.0 - fg
    lf_s[...] = jnp.log(fg)
    v_s[...] = proj[:, 2 * d:3 * d]
    gg = proj[:, 3 * d:]
    gg_s[...] = gg * jax.nn.sigmoid(gg)

    row = lax.broadcasted_iota(jnp.int32, (c, c), 0)
    col = lax.broadcasted_iota(jnp.int32, (c, c), 1)
    causal = row >= col
    tril = causal.astype(F32)

    def chunk(ci, carry):
        r0 = pl.multiple_of(ci * c, c)
        lf = lf_s[pl.ds(r0, c), :]
        qc = q_s[pl.ds(r0, c), :]
        kc = k_s[pl.ds(r0, c), :]
        vc = v_s[pl.ds(r0, c), :].astype(BF16)
        cum = jnp.dot(tril, lf, precision=HIGHEST, preferred_element_type=F32)
        mid = cum[c // 2 - 1:c // 2, :]
        last = cum[c - 1:c, :]
        qh = (qc * jnp.exp(cum - mid)).astype(BF16)
        kh = (kc * jnp.exp(mid - cum)).astype(BF16)
        qs = (qc * jnp.exp(cum)).astype(BF16)
        kl = (kc * jnp.exp(last - cum)).astype(BF16)
        dec = jnp.exp(last)
        sls = [slice(hd * HEAD_DIM, (hd + 1) * HEAD_DIM) for hd in range(nh)]
        dn_t = (((1,), (1,)), ((), ()))
        scs = [lax.dot_general(qh[:, sl], kh[:, sl], dn_t, preferred_element_type=F32) for sl in sls]
        sts = [st_s[hd] for hd in range(nh)]
        ups = [lax.dot_general(vc[:, sl], kl[:, sl], (((0,), (0,)), ((), ())), preferred_element_type=F32)
               for sl in sls]
        o_st = [lax.dot_general(qs[:, sl], st.astype(BF16), dn_t, preferred_element_type=F32)
                for sl, st in zip(sls, sts)]
        scs = [jnp.where(causal, sc, 0.0).astype(BF16) for sc in scs]
        o_in = [jnp.dot(sc, vc[:, sl], preferred_element_type=F32) for sc, sl in zip(scs, sls)]
        for hd, sl in enumerate(sls):
            ob_s[pl.ds(r0, c), sl] = o_in[hd] + o_st[hd]
            st_s[hd] = sts[hd] * dec[:, sl] + ups[hd]
        return carry

    lax.fori_loop(0, ts // c, chunk, 0, unroll=2)

    og = og_ref[...]
    gg = gg_s[...]
    parts = []
    for hd in range(nh):
        sl = slice(hd * HEAD_DIM, (hd + 1) * HEAD_DIM)
        o = ob_s[:, sl]
        o = o * lax.rsqrt(jnp.mean(o * o, axis=-1, keepdims=True) + NORM_EPS)
        parts.append((o * og[:, sl] * gg[:, sl]).astype(BF16))
    on = jnp.concatenate(parts, axis=1)
    mix = jnp.dot(on, wout_ref[...], preferred_element_type=F32)
    o_ref[...] = h + gate_ref[...] * mix


def _hgrn_layer(h, g, sh, sc, gate, w_in, lb_logits, onorm_g, w_out, layer, ts=256):
    b, s, d = h.shape
    nh = d // HEAD_DIM
    row = lambda bi, ti: (bi, 0, 0)
    const2 = lambda bi, ti: (0, 0)
    return pl.pallas_call(
        functools.partial(_hgrn_kernel, layer=layer, ts=ts, nh=nh),
        out_shape=jax.ShapeDtypeStruct((b, s, d), F32),
        grid=(b, s // ts),
        in_specs=[
            pl.BlockSpec((None, ts, d), lambda bi, ti: (bi, ti, 0)),
            pl.BlockSpec((1, d), const2),
            pl.BlockSpec((None, 1, d), row),
            pl.BlockSpec((None, 1, d), row),
            pl.BlockSpec((None, 1, d), row),
            pl.BlockSpec(w_in.shape, const2),
            pl.BlockSpec(lb_logits.shape, const2),
            pl.BlockSpec((1, d), const2),
            pl.BlockSpec(w_out.shape, const2),
        ],
        out_specs=pl.BlockSpec((None, ts, d), lambda bi, ti: (bi, ti, 0)),
        scratch_shapes=[pltpu.VMEM((ts, d), F32)] * 6 + [pltpu.VMEM((nh, HEAD_DIM, HEAD_DIM), F32)],
        compiler_params=_cparams("arbitrary", "arbitrary"),
        name="hgrn2_layer",
    )(h, g, sh, sc, gate, w_in, lb_logits, onorm_g, w_out)


def _proj_heads_kernel(h_ref, g_ref, sh_ref, sc_ref, w_ref, o_ref, *, out_scale):
    hn = _rms_mod(h_ref[...], g_ref[...], sh_ref[...], sc_ref[...])
    y = jnp.dot(hn.astype(BF16), w_ref[...], preferred_element_type=F32)
    if out_scale != 1.0:
        y = y * out_scale
    for j in range(o_ref.shape[0]):
        o_ref[j] = y[:, j * HEAD_DIM:(j + 1) * HEAD_DIM].astype(o_ref.dtype)


def _proj_heads(h, g, sh, sc, w, out_scale=1.0, tm=512):
    b, s, d = h.shape
    tm = min(tm, s)
    n = w.shape[1]
    nj = n // HEAD_DIM
    row = lambda bi, ti: (bi, 0, 0)
    const2 = lambda bi, ti: (0, 0)
    return pl.pallas_call(
        functools.partial(_proj_heads_kernel, out_scale=out_scale),
        out_shape=jax.ShapeDtypeStruct((b, nj, s, HEAD_DIM), BF16),
        grid=(b, s // tm),
        in_specs=[
            pl.BlockSpec((None, tm, d), lambda bi, ti: (bi, ti, 0)),
            pl.BlockSpec((1, d), const2),
            pl.BlockSpec((None, 1, d), row),
            pl.BlockSpec((None, 1, d), row),
            pl.BlockSpec(w.shape, const2),
        ],
        out_specs=pl.BlockSpec((None, nj, tm, HEAD_DIM), lambda bi, ti: (bi, 0, ti, 0)),
        compiler_params=_cparams("arbitrary", "arbitrary"),
        name="norm_mod_proj",
    )(h, g, sh, sc, w)


SB_HEADS_PER_STEP = 2
SB_QTILE = 2048
SB_AHEAD = 2
SB_QGROUP = 4


def _sb_attn_kernel(q_ref, k_ref, v_ref, o_ref, kn_ref, *, s, tq):
    qb = SB_BLOCK
    nh = q_ref.shape[0]
    row = lax.broadcasted_iota(jnp.int32, (qb, qb), 0)
    col = lax.broadcasted_iota(jnp.int32, (qb, qb), 1)
    strict = col < row
    tri = (row >= col).astype(BF16)
    ones = jnp.ones((qb, qb), BF16)

    @pl.when(pl.program_id(2) == 0)
    def _():
        rows = min(s, 8 * qb)
        for hd in range(nh):
            def kmax_body(j, m, hd=hd):
                kb = k_ref[hd, pl.ds(pl.multiple_of(j * rows, rows), rows), :].astype(F32)
                ssq = jnp.dot((kb * kb).astype(BF16), ones, preferred_element_type=F32)
                return jnp.maximum(m, jnp.max(ssq, axis=0, keepdims=True))
            m = lax.fori_loop(0, s // rows, kmax_body, jnp.zeros((1, qb), F32))
            kn_ref[hd] = jnp.sqrt(jnp.max(m)) * 1.01

    def scores(hd, q, j, keep):
        r0 = pl.multiple_of(j * qb, qb)
        kb = k_ref[hd, pl.ds(r0, qb), :]
        vb = v_ref[hd, pl.ds(r0, qb), :]
        z = lax.dot_general(q, kb, (((1,), (1,)), ((), ())), preferred_element_type=F32)
        lk = jax.nn.log_sigmoid(-z)
        if keep is not None:
            lk = jnp.where(keep, lk, 0.0)
        hi = lk.astype(BF16)
        lo = (lk - hi.astype(F32)).astype(BF16)
        cs = (jnp.dot(hi, tri, preferred_element_type=F32)
              + jnp.dot(lo, tri, preferred_element_type=F32))
        return z, cs, vb

    def accumulate(z, cs, vb, keep, after, acc):
        a = jnp.exp(z + cs + after)
        if keep is not None:
            a = jnp.where(keep, a, 0.0)
        acc = acc + jnp.dot(a.astype(BF16), vb, preferred_element_type=F32)
        return after + cs[:, 0:1], acc

    def block(hd, q, j, after, acc):
        z, cs, vb = scores(hd, q, j, None)
        return accumulate(z, cs, vb, None, after, acc)

    def head_of_walk(q0, gi):
        qs, zbounds, afters, accs = [], [], [], []
        for hd in range(nh):
            q = q_ref[hd, pl.ds(q0, qb), :]
            qf = q.astype(F32)
            zbounds.append(jnp.sqrt(jnp.sum(qf * qf, axis=1, keepdims=True)) * kn_ref[hd])
            qs.append(q)
        for hd in range(nh):
            keeps = [strict] + [gi - a >= 0 for a in range(1, SB_AHEAD + 1)]
            parts = [scores(hd, qs[hd], jnp.maximum(gi - a, 0), keeps[a]) for a in range(SB_AHEAD + 1)]
            after, acc = jnp.zeros((qb, 1), F32), jnp.zeros((qb, HEAD_DIM), F32)
            for (z, cs, vb), keep in zip(parts, keeps):
                after, acc = accumulate(z, cs, vb, keep, after, acc)
            afters.append(after)
            accs.append(acc)
        return qs, zbounds, afters, accs

    def qblocks(i, carry):
        pos = []
        for u in range(SB_QGROUP):
            q0 = pl.multiple_of((i * SB_QGROUP + u) * qb, qb)
            gi = pl.program_id(2) * (tq // qb) + i * SB_QGROUP + u
            pos.append((q0, gi))
        chains = [(w, hd, a) for w in range(SB_QGROUP) for hd in range(nh) for a in range(SB_AHEAD + 1)]
        qs = {(w, hd): q_ref[hd, pl.ds(pos[w][0], qb), :] for w in range(SB_QGROUP) for hd in range(nh)}
        keep = {(w, a): (strict if a == 0 else pos[w][1] - a >= 0) for w in range(SB_QGROUP)
                for a in range(SB_AHEAD + 1)}
        kv, z, lk, cs = {}, {}, {}, {}
        for w, hd, a in chains:
            r0 = pl.multiple_of(jnp.maximum(pos[w][1] - a, 0) * qb, qb)
            kv[w, hd, a] = (k_ref[hd, pl.ds(r0, qb), :], v_ref[hd, pl.ds(r0, qb), :])
            z[w, hd, a] = lax.dot_general(qs[w, hd], kv[w, hd, a][0], (((1,), (1,)), ((), ())),
                                          preferred_element_type=F32)
        for c in chains:
            lk[c] = jnp.where(keep[c[0], c[2]], jax.nn.log_sigmoid(-z[c]), 0.0)
        for c in chains:
            hi = lk[c].astype(BF16)
            lo = (lk[c] - hi.astype(F32)).astype(BF16)
            cs[c] = (jnp.dot(hi, tri, preferred_element_type=F32)
                     + jnp.dot(lo, tri, preferred_element_type=F32))
        walks = []
        for w in range(SB_QGROUP):
            zbounds, afters, accs = [], [], []
            for hd in range(nh):
                qf = qs[w, hd].astype(F32)
                zbounds.append(jnp.sqrt(jnp.sum(qf * qf, axis=1, keepdims=True)) * kn_ref[hd])
                after, acc = jnp.zeros((qb, 1), F32), jnp.zeros((qb, HEAD_DIM), F32)
                for a in range(SB_AHEAD + 1):
                    after, acc = accumulate(z[w, hd, a], cs[w, hd, a], kv[w, hd, a][1], keep[w, a], after, acc)
                afters.append(after)
                accs.append(acc)
            walks.append((pos[w][0], pos[w][1], [qs[w, hd] for hd in range(nh)], zbounds, afters, accs))
        def cond(st):
            step, afters, _ = st
            live = None
            for w, (_, gi, _, zbounds, _, _) in enumerate(walks):
                for hd in range(nh):
                    m = jnp.where(gi - SB_AHEAD - 1 - step >= 0, jnp.max(zbounds[hd] + afters[w][hd]), -jnp.inf)
                    live = m if live is None else jnp.maximum(live, m)
            return live > SB_LOG_FLOOR

        def body(st):
            step, afters, accs = st
            new_afters, new_accs = [], []
            for w, (_, gi, qs_w, _, _, _) in enumerate(walks):
                j = gi - SB_AHEAD - 1 - step
                inside = j >= 0
                out = []
                for hd in range(nh):
                    z, cs, vb = scores(hd, qs_w[hd], jnp.maximum(j, 0), inside)
                    out.append(accumulate(z, cs, vb, inside, afters[w][hd], accs[w][hd]))
                new_afters.append(tuple(o[0] for o in out))
                new_accs.append(tuple(o[1] for o in out))
            return step + 1, tuple(new_afters), tuple(new_accs)

        init = (jnp.int32(0), tuple(tuple(w[4]) for w in walks), tuple(tuple(w[5]) for w in walks))
        _, _, accs = lax.while_loop(cond, body, init)
        for w, (q0, _, _, _, _, _) in enumerate(walks):
            for hd in range(nh):
                o_ref[hd, pl.ds(q0, qb), :] = accs[w][hd].astype(o_ref.dtype)
        return carry

    lax.fori_loop(0, tq // (qb * SB_QGROUP), qblocks, 0)


def _sb_attention(q, kv):
    b, nh, s, dh = q.shape
    hp = SB_HEADS_PER_STEP
    tq = min(SB_QTILE, s)
    qblk = (None, hp, tq, dh)
    kblk = (None, hp, s, dh)
    return pl.pallas_call(
        functools.partial(_sb_attn_kernel, s=s, tq=tq),
        out_shape=jax.ShapeDtypeStruct((b, nh, s, dh), BF16),
        grid=(b, nh // hp, s // tq),
        in_specs=[
            pl.BlockSpec(qblk, lambda bi, hi, qi: (bi, hi, qi, 0)),
            pl.BlockSpec(kblk, lambda bi, hi, qi: (bi, hi, 0, 0)),
            pl.BlockSpec(kblk, lambda bi, hi, qi: (bi, nh // hp + hi, 0, 0)),
        ],
        out_specs=pl.BlockSpec(qblk, lambda bi, hi, qi: (bi, hi, qi, 0)),
        scratch_shapes=[pltpu.SMEM((hp,), F32)],
        compiler_params=_cparams("arbitrary", "arbitrary", "arbitrary"),
        name="stick_breaking_attention",
    )(q, kv, kv)


def _out_proj_kernel(a_ref, w_ref, h_ref, gate_ref, o_ref):
    a = jnp.concatenate([a_ref[j] for j in range(a_ref.shape[0])], axis=1)
    mix = jnp.dot(a, w_ref[...], preferred_element_type=F32)
    o_ref[...] = h_ref[...] + gate_ref[...] * mix


def _out_proj_residual(a, w, h, gate, tm=512):
    b, s, d = h.shape
    tm = min(tm, s)
    nh = a.shape[1]
    return pl.pallas_call(
        _out_proj_kernel,
        out_shape=jax.ShapeDtypeStruct((b, s, d), F32),
        grid=(b, s // tm),
        in_specs=[
            pl.BlockSpec((None, nh, tm, HEAD_DIM), lambda bi, ti: (bi, 0, ti, 0)),
            pl.BlockSpec(w.shape, lambda bi, ti: (0, 0)),
            pl.BlockSpec((None, tm, d), lambda bi, ti: (bi, ti, 0)),
            pl.BlockSpec((None, 1, d), lambda bi, ti: (bi, 0, 0)),
        ],
        out_specs=pl.BlockSpec((None, tm, d), lambda bi, ti: (bi, ti, 0)),
        compiler_params=_cparams("arbitrary", "arbitrary"),
        name="out_proj_residual",
    )(a, w, h, gate)


PEER_RING = 4
PEER_GROUP = 8
PEER_ROWS = 16
PEER_PITCH = 20

def _top16(lists):
    ss = [s for s, _ in lists]
    iotas = {}
    for s in ss:
        if s.shape not in iotas:
            iotas[s.shape] = lax.broadcasted_iota(jnp.int32, s.shape, 0).astype(F32)
    vals, picked = [[] for _ in lists], [[] for _ in lists]
    for _ in range(PEER_TOPK):
        for li, (_, ids) in enumerate(lists):
            s, iota = ss[li], iotas[ss[li].shape]
            m = jnp.max(s, axis=0, keepdims=True)
            first = jnp.min(jnp.where(s == m, iota, float(s.shape[0])), axis=0, keepdims=True)
            hit = iota == first
            vals[li].append(m)
            picked[li].append(first if ids is None else jnp.sum(jnp.where(hit, ids, 0), axis=0, keepdims=True))
            ss[li] = jnp.where(hit, -jnp.inf, s)
    return [(jnp.concatenate(v, axis=0), jnp.concatenate(p, axis=0).astype(jnp.int32))
            for v, p in zip(vals, picked)]


_PAIR_COUNTS = tuple(PEER_TOPK // (i + 1) for i in range(PEER_TOPK))
_PAIR_PAD = -sum(_PAIR_COUNTS) % 8
ROUTE_HEADS = 1


def _route_kernel(h_ref, g_ref, sh_ref, sc_ref, wqh_ref, wql_ref, sk_ref, e_ref, gt_ref, *, row_base, tb):
    hn = _rms_mod(h_ref[...], g_ref[...], sh_ref[...], sc_ref[...])
    hn_hi = hn.astype(BF16)
    hn_lo = (hn - hn_hi.astype(F32)).astype(BF16)
    qr = (jnp.dot(hn_hi, wqh_ref[...], preferred_element_type=F32)
          + (jnp.dot(hn_hi, wql_ref[...], preferred_element_type=F32)
             + jnp.dot(hn_lo, wqh_ref[...], preferred_element_type=F32)))
    tm = qr.shape[0]
    for h0 in range(0, PEER_HEADS, ROUTE_HEADS):
        heads = range(h0, h0 + ROUTE_HEADS)
        scores = []
        for hd in heads:
            for p in range(2):
                c0 = (hd * 2 + p) * HEAD_DIM
                scores.append(lax.dot_general(sk_ref[hd, p], qr[:, c0:c0 + HEAD_DIM], (((1,), (1,)), ((), ())),
                                              precision=HIGHEST, preferred_element_type=F32))
        tops = _top16([(sc, None) for sc in scores])
        pairs = []
        for k in range(ROUTE_HEADS):
            (s1, i1), (s2, i2) = tops[2 * k], tops[2 * k + 1]
            cand = jnp.concatenate([s1[i:i + 1, :] + s2[:c, :] for i, c in enumerate(_PAIR_COUNTS)]
                                   + [jnp.full((_PAIR_PAD, tm), -jnp.inf, F32)], axis=0)
            eid = jnp.concatenate([i1[i:i + 1, :] * PEER_NKEYS + i2[:c, :] for i, c in enumerate(_PAIR_COUNTS)]
                                  + [jnp.zeros((_PAIR_PAD, tm), jnp.int32)], axis=0)
            pairs.append((cand, eid))
        for hd, (best, experts) in zip(heads, _top16(pairs)):
            ex = jnp.exp(best - best[0:1, :])
            gates = ex / jnp.sum(ex, axis=0, keepdims=True)
            first_row = (experts + row_base) * PEER_ROWS
            for j in range(tm // tb):
                e_ref[j, hd * PEER_TOPK:(hd + 1) * PEER_TOPK, :] = first_row[:, j * tb:(j + 1) * tb]
                gt_ref[j, hd * PEER_TOPK:(hd + 1) * PEER_TOPK, :] = gates[:, j * tb:(j + 1) * tb]


def _peer_route(h2, g, sh, sc, w_q, subkeys, s, row_base, tm=128):
    t, d = h2.shape
    tm = min(tm, s)
    tb = PEER_RING * PEER_GROUP
    ne = PEER_HEADS * PEER_TOPK
    per_b = s // tm
    row = lambda i: (i // per_b, 0, 0)
    w_hi = w_q.astype(BF16)
    w_lo = (w_q - w_hi.astype(F32)).astype(BF16)
    return pl.pallas_call(
        functools.partial(_route_kernel, row_base=row_base, tb=tb),
        out_shape=(jax.ShapeDtypeStruct((t // tb, ne, tb), jnp.int32), jax.ShapeDtypeStruct((t // tb, ne, tb), F32)),
        grid=(t // tm,),
        in_specs=[
            pl.BlockSpec((tm, d), lambda i: (i, 0)),
            pl.BlockSpec((1, d), lambda i: (0, 0)),
            pl.BlockSpec((None, 1, d), row),
            pl.BlockSpec((None, 1, d), row),
            pl.BlockSpec(w_q.shape, lambda i: (0, 0)),
            pl.BlockSpec(w_q.shape, lambda i: (0, 0)),
            pl.BlockSpec(subkeys.shape, lambda i: (0, 0, 0, 0)),
        ],
        out_specs=(pl.BlockSpec((tm // tb, ne, tb), lambda i: (i, 0, 0)),
                   pl.BlockSpec((tm // tb, ne, tb), lambda i: (i, 0, 0))),
        compiler_params=_cparams("arbitrary"),
        name="peer_route",
    )(h2, g, sh, sc, w_hi, w_lo, subkeys)


def _pack_kernel(u_ref, v_ref, o_ref):
    tm, d = u_ref.shape
    nc = d // HEAD_DIM
    for src, base in ((u_ref, 0), (v_ref, nc)):
        for c in range(nc):
            o_ref[pl.ds(base + c, tm, stride=PEER_ROWS), :] = src[:, c * HEAD_DIM:(c + 1) * HEAD_DIM]


def _pack_experts(u, v, tm=256):
    n, d = u.shape
    assert 2 * d == PEER_ROWS * HEAD_DIM
    return pl.pallas_call(
        _pack_kernel,
        out_shape=jax.ShapeDtypeStruct((n * PEER_ROWS, HEAD_DIM), F32),
        grid=(n // tm,),
        in_specs=[pl.BlockSpec((tm, d), lambda i: (i, 0)), pl.BlockSpec((tm, d), lambda i: (i, 0))],
        out_specs=pl.BlockSpec((tm * PEER_ROWS, HEAD_DIM), lambda i: (i, 0)),
        compiler_params=_cparams("arbitrary"),
        name="peer_pack_experts",
    )(u, v)


def _expert_kernel(idx_ref, idxn_ref, gt_ref, h_ref, g_ref, sh_ref, sc_ref, gate_ref, fg_ref, tab_ref,
                   o_ref, *scratch, tg, ne, final_norm):
    bufs, sem = scratch[:PEER_RING], scratch[PEER_RING]
    i = pl.program_id(0)
    n = pl.num_programs(0)
    d = h_ref.shape[-1]
    rows = tg * ne
    ahead = PEER_RING // 2

    nc = d // HEAD_DIM
    rc = PEER_ROWS

    def start_token(ids_ref, grp, t, k):
        for e in range(ne):
            src = tab_ref.at[pl.ds(pl.multiple_of(ids_ref[e, grp * tg + t], rc), rc), :]
            pltpu.make_async_copy(src, bufs[k].at[pl.ds((t * ne + e) * PEER_PITCH, rc), :],
                                  sem.at[k]).start(priority=e % 2)

    def wait_group(k):
        pltpu.make_async_copy(tab_ref.at[pl.ds(0, rows * rc)], bufs[k].at[pl.ds(0, rows * rc)], sem.at[k]).wait()

    @pl.when(i == 0)
    def _():
        for k in range(ahead):
            for t in range(tg):
                start_token(idx_ref, k, t, k)

    h = h_ref[...]
    hn = _rms_mod(h, g_ref[...], sh_ref[...], sc_ref[...])
    gt = gt_ref[...]
    outs = []
    for k in range(PEER_RING):
        wait_group(k)
        nk = (k + ahead) % PEER_RING
        ids_ref = idx_ref if k + ahead < PEER_RING else idxn_ref
        for t in range(tg):
            start_token(ids_ref, nk, t, nk)
            tok = k * tg + t

            def chunk(g, c):
                return bufs[k][pl.ds(((t * ne + g * 8) * PEER_PITCH + c), 8, stride=PEER_PITCH), :]

            ws = []
            for g in range(ne // 8):
                acc = chunk(g, 0) * hn[tok:tok + 1, 0:HEAD_DIM]
                for c in range(1, nc):
                    acc = acc + chunk(g, c) * hn[tok:tok + 1, c * HEAD_DIM:(c + 1) * HEAD_DIM]
                hid = jnp.sum(acc, axis=1, keepdims=True)
                ws.append(gt[g * 8:(g + 1) * 8, tok:tok + 1] * _gelu(hid))
            pieces = []
            for c in range(nc):
                acc = ws[0] * chunk(0, nc + c)
                for g in range(1, ne // 8):
                    acc = acc + ws[g] * chunk(g, nc + c)
                pieces.append(jnp.sum(acc, axis=0, keepdims=True))
            outs.append(jnp.concatenate(pieces, axis=1))
    y = h + gate_ref[...] * jnp.concatenate(outs, axis=0)
    if final_norm:
        y = y * lax.rsqrt(jnp.mean(y * y, axis=-1, keepdims=True) + NORM_EPS) * fg_ref[...]
    o_ref[...] = y

    @pl.when(i == n - 1)
    def _():
        for k in range(ahead):
            wait_group(k)


def _peer_experts(h2, experts, gates, g, sh, sc, gate, final_g, table, s, final_norm):
    t, d = h2.shape
    assert 2 * d == PEER_ROWS * HEAD_DIM
    nblk, ne, tb = experts.shape
    tg = PEER_GROUP
    per_b = s // tb
    row = lambda i: (i // per_b, 0, 0)
    return pl.pallas_call(
        functools.partial(_expert_kernel, tg=tg, ne=ne, final_norm=final_norm),
        out_shape=jax.ShapeDtypeStruct((t, d), F32),
        grid=(nblk,),
        in_specs=[
            pl.BlockSpec((None, ne, tb), lambda i: (i, 0, 0), memory_space=pltpu.SMEM),
            pl.BlockSpec((None, ne, tb), lambda i: (jnp.minimum(i + 1, nblk - 1), 0, 0), memory_space=pltpu.SMEM),
            pl.BlockSpec((None, ne, tb), lambda i: (i, 0, 0)),
            pl.BlockSpec((tb, d), lambda i: (i, 0)),
            pl.BlockSpec((1, d), lambda i: (0, 0)),
            pl.BlockSpec((None, 1, d), row),
            pl.BlockSpec((None, 1, d), row),
            pl.BlockSpec((None, 1, d), row),
            pl.BlockSpec((1, d), lambda i: (0, 0)),
            pl.BlockSpec(memory_space=pl.ANY),
        ],
        out_specs=pl.BlockSpec((tb, d), lambda i: (i, 0)),
        scratch_shapes=([pltpu.VMEM((tg * ne * PEER_PITCH, HEAD_DIM), F32)] * PEER_RING
                        + [pltpu.SemaphoreType.DMA((PEER_RING,))]),
        compiler_params=_cparams("arbitrary"),
        name="peer_experts",
    )(experts, experts, gates, h2, g, sh, sc, gate, final_g, table)


def kernel(x, c, ada_w, ada_b, norm_mix_g, norm_ffn_g, hgrn_w_in, hgrn_lb_logits, hgrn_onorm_g, hgrn_w_out,
           kv_ada_w, kv_ada_b, kv_norm_g, kv_w, sb_w_q, sb_w_out, peer_w_q, peer_subkeys, peer_u, peer_v,
           final_norm_g):
    b, s, d = x.shape
    depth = ada_w.shape[0]
    n_a = hgrn_w_in.shape[0]
    c8 = jnp.zeros((8, d), F32).at[:b].set(c)

    def mods(w3, b3, layer, parts):
        m = _modulation(c8, w3, b3.reshape(b3.shape[0], 1, -1), layer)[:b]
        return [p.reshape(b, 1, d) for p in jnp.split(m, parts, axis=-1)]

    n_exp = peer_u.shape[1]

    table = _pack_experts(peer_u.reshape(depth * n_exp, d), peer_v.reshape(depth * n_exp, d))
    h = x
    kv = None
    for l in range(depth):
        sh1, sc1, g1, sh2, sc2, g2 = mods(ada_w, ada_b, l, 6)
        gm = norm_mix_g[l].reshape(1, d)
        if l < n_a:
            h = _hgrn_layer(h, gm, sh1, sc1, g1, hgrn_w_in[l].astype(BF16), hgrn_lb_logits,
                            hgrn_onorm_g[l].reshape(1, d), hgrn_w_out[l].astype(BF16), l)
        else:
            q = _proj_heads(h, gm, sh1, sc1, sb_w_q[l - n_a].astype(BF16),
                            out_scale=1.0 / math.sqrt(HEAD_DIM))
            a = _sb_attention(q, kv)
            h = _out_proj_residual(a, sb_w_out[l - n_a].astype(BF16), h, g1)
        gf = norm_ffn_g[l].reshape(1, d)
        h2 = h.reshape(b * s, d)
        experts, gates = _peer_route(h2, gf, sh2, sc2, peer_w_q[l], peer_subkeys[l], s, row_base=l * n_exp)
        h2 = _peer_experts(h2, experts, gates, gf, sh2, sc2, g2, final_norm_g.reshape(1, d), table, s,
                           final_norm=(l == depth - 1))
        h = h2.reshape(b, s, d)
        if l == n_a - 1:
            ksh, ksc = mods(kv_ada_w.reshape(1, d, 2 * d), kv_ada_b.reshape(1, 2 * d), 0, 2)
            kv = _proj_heads(h, kv_norm_g.reshape(1, d), ksh, ksc, kv_w.astype(BF16))
    return h
```
